```python
import jax, jax.numpy as jnp
from jax import lax
import numpy as np

D_MODEL = 1024
BATCH = 8
SEQ = 4096
DEPTH = 2

GRID_W = 64
CTX_LEN = 256
D_FF = 2816
FFN_RES = 0.5
N_MOD = 9
EPS = 1e-6
CONV_CH = 512
CONV_GROUPS = 8
CONV_WIDTH = 31
NA_HEADS = 8
NA_HEAD_DIM = 64
NA_WIDTH = NA_HEADS * NA_HEAD_DIM
NA_WIN_R = 8
NA_WIN_C = 16
EVEN_IN = 2 * CONV_CH + 3 * NA_WIDTH
MIX_WIDTH = CONV_CH + NA_WIDTH
SC_WIDTH = D_MODEL
SC_CONV = 3
N_EVEN = (DEPTH + 1) // 2
N_ODD = DEPTH // 2

kernel_name = "hybrid_conformer_natten_shortconv_dit"


def rms_norm(x, g):
    xf = x.astype(jnp.float32)
    y = xf * lax.rsqrt(jnp.mean(xf * xf, axis=-1, keepdims=True) + EPS)
    return (y * g.astype(jnp.float32)).astype(x.dtype)


def ada_mod(cond, w, b):
    m = jax.nn.silu(cond) @ w + b
    return m.reshape(cond.shape[0], N_MOD, D_MODEL)


def pre(x, m, k, g):
    return rms_norm(x, g) * (1 + m[:, 3 * k + 1, None]) + m[:, 3 * k, None]


def post(y, m, k, g):
    return m[:, 3 * k + 2, None] * rms_norm(y, g)


def swiglu(h, w_gu, w_down):
    gt, up = jnp.split(h @ w_gu, 2, axis=-1)
    return (jax.nn.silu(gt) * up) @ w_down


def dw_conv(x, w):
    k = w.shape[0]
    return lax.conv_general_dilated(
        x, w[:, None, :].astype(x.dtype), window_strides=(1,),
        padding=[(k // 2, k // 2)], dimension_numbers=("NWC", "WIO", "NWC"),
        feature_group_count=x.shape[-1])


def conformer_conv(u, dw_w, dw_b, ln_g, ln_b):
    a, gt = jnp.split(u, 2, axis=-1)
    v = dw_conv(a * jax.nn.sigmoid(gt), dw_w) + dw_b
    vg = v.reshape(v.shape[:-1] + (CONV_GROUPS, CONV_CH // CONV_GROUPS)).astype(jnp.float32)
    mu = jnp.mean(vg, axis=-1, keepdims=True)
    var = jnp.mean(jnp.square(vg - mu), axis=-1, keepdims=True)
    vn = ((vg - mu) * lax.rsqrt(var + EPS)).reshape(v.shape)
    vn = vn * ln_g.astype(jnp.float32) + ln_b.astype(jnp.float32)
    return jax.nn.silu(vn).astype(u.dtype)


def ctx_attention(q, k, v):
    s = jnp.einsum('bqhd,bkhd->bhqk', q, k).astype(jnp.float32) * (NA_HEAD_DIM ** -0.5)
    p = jax.nn.softmax(s, axis=-1).astype(v.dtype)
    return jnp.einsum('bhqk,bkhd->bqhd', p, v)


def neighbourhood_attention(q, k, v, k_ctx, v_ctx, rpb):
    bsz, s, h, dh = q.shape
    rows = s // GRID_W
    kr = min(NA_WIN_R, rows)
    kc = NA_WIN_C
    qg = q.reshape(bsz, rows, GRID_W, h, dh)
    kg = k.reshape(bsz, rows, GRID_W, h, dh)
    vg = v.reshape(bsz, rows, GRID_W, h, dh)
    row_start = jnp.clip(jnp.arange(rows) - kr // 2, 0, rows - kr)
    cols = jnp.arange(GRID_W)
    col_start = jnp.clip(cols - kc // 2, 0, GRID_W - kc)
    col_idx = col_start[:, None] + jnp.arange(kc)[None, :]
    col_bias_idx = col_idx - cols[:, None] + (NA_WIN_C - 1)
    scale = NA_HEAD_DIM ** -0.5

    def one_row(r):
        q_r = lax.dynamic_index_in_dim(qg, r, axis=1, keepdims=False)
        rs = row_start[r]
        k_rows = lax.dynamic_slice_in_dim(kg, rs, kr, axis=1)
        v_rows = lax.dynamic_slice_in_dim(vg, rs, kr, axis=1)
        k_win = k_rows[:, :, col_idx]
        v_win = v_rows[:, :, col_idx]
        row_bias_idx = rs + jnp.arange(kr) - r + (NA_WIN_R - 1)
        bias = rpb[:, row_bias_idx][:, :, col_bias_idx]
        bias = jnp.transpose(bias, (0, 2, 1, 3)).astype(jnp.float32)
        s_win = jnp.einsum('bwhd,brwjhd->bhwrj', q_r, k_win).astype(jnp.float32) * scale + bias
        s_ctx = jnp.einsum('bwhd,bchd->bhwc', q_r, k_ctx).astype(jnp.float32) * scale
        logits = jnp.concatenate([s_win.reshape(bsz, h, GRID_W, kr * kc), s_ctx], axis=-1)
        p = jax.nn.softmax(logits, axis=-1).astype(v.dtype)
        p_win = p[..., :kr * kc].reshape(bsz, h, GRID_W, kr, kc)
        p_ctx = p[..., kr * kc:]
        return (jnp.einsum('bhwrj,brwjhd->bwhd', p_win, v_win)
                + jnp.einsum('bhwc,bchd->bwhd', p_ctx, v_ctx))

    out = lax.map(one_row, jnp.arange(rows))
    return jnp.moveaxis(out, 0, 1).reshape(bsz, s, h * dh)


def even_mixer(h_lat, h_ctx, w_in, w_out, dw_w, dw_b, ln_g, ln_b, rpb, ctx_out):
    bsz, s, _ = h_lat.shape
    lc = h_ctx.shape[1]
    u = h_lat @ w_in
    y_a = conformer_conv(u[..., :2 * CONV_CH], dw_w, dw_b, ln_g, ln_b)
    q, k, v = jnp.split(u[..., 2 * CONV_CH:], 3, axis=-1)
    q = q.reshape(bsz, s, NA_HEADS, NA_HEAD_DIM)
    k = k.reshape(bsz, s, NA_HEADS, NA_HEAD_DIM)
    v = v.reshape(bsz, s, NA_HEADS, NA_HEAD_DIM)
    y_ctx = None
    if ctx_out:
        u_c = h_ctx @ w_in
        y_a_c = conformer_conv(u_c[..., :2 * CONV_CH], dw_w, dw_b, ln_g, ln_b)
        q_c, k_c, v_c = jnp.split(u_c[..., 2 * CONV_CH:], 3, axis=-1)
        q_c = q_c.reshape(bsz, lc, NA_HEADS, NA_HEAD_DIM)
        k_c = k_c.reshape(bsz, lc, NA_HEADS, NA_HEAD_DIM)
        v_c = v_c.reshape(bsz, lc, NA_HEADS, NA_HEAD_DIM)
        y_b_c = ctx_attention(q_c, k_c, v_c).reshape(bsz, lc, NA_WIDTH)
        y_ctx = jnp.concatenate([y_a_c, y_b_c], axis=-1) @ w_out
    else:
        k_c, v_c = jnp.split(h_ctx @ w_in[:, 2 * CONV_CH + NA_WIDTH:], 2, axis=-1)
        k_c = k_c.reshape(bsz, lc, NA_HEADS, NA_HEAD_DIM)
        v_c = v_c.reshape(bsz, lc, NA_HEADS, NA_HEAD_DIM)
    y_b = neighbourhood_attention(q, k, v, k_c, v_c, rpb)
    y_lat = jnp.concatenate([y_a, y_b], axis=-1) @ w_out
    return y_lat, y_ctx


def short_conv_mixer(h, w_in, w_conv, w_out):
    b_gate, c_gate, xv = jnp.split(h @ w_in, 3, axis=-1)
    return (b_gate * dw_conv(c_gate * xv, w_conv)) @ w_out


def setup_inputs(seed: int = 0) -> dict:
    key = jax.random.key(seed)
    ks = jax.random.split(key, 24)
    nrm = lambda k, shp, s: jax.random.normal(k, shp, jnp.float32) * s
    return {
        "x": nrm(ks[0], (BATCH, SEQ, D_MODEL), 1.0),
        "c": nrm(ks[1], (BATCH, D_MODEL), 1.0),
        "ctx": nrm(ks[2], (BATCH, CTX_LEN, D_MODEL), 1.0),
        "c_ctx": nrm(ks[3], (D_MODEL,), 1.0),
        "w_mod": nrm(ks[4], (DEPTH, D_MODEL, N_MOD * D_MODEL), 0.5 * D_MODEL ** -0.5),
        "b_mod": nrm(ks[5], (DEPTH, N_MOD * D_MODEL), 0.01),
        "norm_g": 1.0 + nrm(ks[6], (DEPTH, 6, D_MODEL), 0.05),
        "ff1_w_gu": nrm(ks[7], (DEPTH, D_MODEL, 2 * D_FF), D_MODEL ** -0.5),
        "ff1_w_down": nrm(ks[8], (DEPTH, D_FF, D_MODEL), D_FF ** -0.5),
        "ff2_w_gu": nrm(ks[9], (DEPTH, D_MODEL, 2 * D_FF), D_MODEL ** -0.5),
        "ff2_w_down": nrm(ks[10], (DEPTH, D_FF, D_MODEL), D_FF ** -0.5),
        "ev_w_in": nrm(ks[11], (N_EVEN, D_MODEL, EVEN_IN), D_MODEL ** -0.5),
        "ev_w_out": nrm(ks[12], (N_EVEN, MIX_WIDTH, D_MODEL), MIX_WIDTH ** -0.5),
        "ev_dw_w": nrm(ks[13], (N_EVEN, CONV_WIDTH, CONV_CH), CONV_WIDTH ** -0.5),
        "ev_dw_b": nrm(ks[14], (N_EVEN, CONV_CH), 0.01),
        "ev_ln_g": 1.0 + nrm(ks[15], (N_EVEN, CONV_CH), 0.05),
        "ev_ln_b": nrm(ks[16], (N_EVEN, CONV_CH), 0.01),
        "ev_rpb": nrm(ks[17], (N_EVEN, NA_HEADS, 2 * NA_WIN_R - 1, 2 * NA_WIN_C - 1), 0.1),
        "od_w_in": nrm(ks[18], (N_ODD, D_MODEL, 3 * SC_WIDTH), D_MODEL ** -0.5),
        "od_conv_w": nrm(ks[19], (N_ODD, SC_CONV, SC_WIDTH), SC_CONV ** -0.5),
        "od_w_out": nrm(ks[20], (N_ODD, SC_WIDTH, D_MODEL), SC_WIDTH ** -0.5),
    }


def reference(x, c, ctx, c_ctx, w_mod, b_mod, norm_g, ff1_w_gu, ff1_w_down, ff2_w_gu, ff2_w_down,
              ev_w_in, ev_w_out, ev_dw_w, ev_dw_b, ev_ln_g, ev_ln_b, ev_rpb,
              od_w_in, od_conv_w, od_w_out):
    x_lat, x_ctx = x, ctx
    for i in range(DEPTH):
        ctx_in = any(j % 2 == 0 for j in range(i, DEPTH))
        ctx_out = any(j % 2 == 0 for j in range(i + 1, DEPTH))
        g = norm_g[i]
        m_lat = ada_mod(c, w_mod[i], b_mod[i])
        m_ctx = ada_mod(c_ctx[None], w_mod[i], b_mod[i])

        x_lat = x_lat + FFN_RES * post(swiglu(pre(x_lat, m_lat, 0, g[0]), ff1_w_gu[i], ff1_w_down[i]), m_lat, 0, g[1])
        if ctx_in:
            x_ctx = x_ctx + FFN_RES * post(swiglu(pre(x_ctx, m_ctx, 0, g[0]), ff1_w_gu[i], ff1_w_down[i]), m_ctx, 0, g[1])

        if i % 2 == 0:
            e = i // 2
            h_l = pre(x_lat, m_lat, 1, g[2])
            h_c = pre(x_ctx, m_ctx, 1, g[2])
            y_l, y_c = even_mixer(h_l, h_c, ev_w_in[e], ev_w_out[e], ev_dw_w[e], ev_dw_b[e],
                                  ev_ln_g[e], ev_ln_b[e], ev_rpb[e], ctx_out)
            x_lat = x_lat + post(y_l, m_lat, 1, g[3])
            if ctx_out:
                x_ctx = x_ctx + post(y_c, m_ctx, 1, g[3])
        else:
            o = i // 2
            x_lat = x_lat + post(short_conv_mixer(pre(x_lat, m_lat, 1, g[2]), od_w_in[o], od_conv_w[o], od_w_out[o]), m_lat, 1, g[3])
            if ctx_out:
                x_ctx = x_ctx + post(short_conv_mixer(pre(x_ctx, m_ctx, 1, g[2]), od_w_in[o], od_conv_w[o], od_w_out[o]), m_ctx, 1, g[3])

        x_lat = x_lat + FFN_RES * post(swiglu(pre(x_lat, m_lat, 2, g[4]), ff2_w_gu[i], ff2_w_down[i]), m_lat, 2, g[5])
        if ctx_out:
            x_ctx = x_ctx + FFN_RES * post(swiglu(pre(x_ctx, m_ctx, 2, g[4]), ff2_w_gu[i], ff2_w_down[i]), m_ctx, 2, g[5])
    return x_lat
```

```python
import functools

import jax
import jax.numpy as jnp
import numpy as np
from jax import lax
from jax.experimental import pallas as pl
from jax.experimental.pallas import tpu as pltpu

D_MODEL = 1024
D_FF = 2816
N_MOD = 9
EPS = 1e-6
FFN_RES = 0.5
GRID_W = 64
CONV_CH = 512
CONV_GROUPS = 8
CONV_WIDTH = 31
NA_HEADS = 8
NA_HEAD_DIM = 64
NA_WIDTH = NA_HEADS * NA_HEAD_DIM
NA_WIN_R = 8
NA_WIN_C = 16
SC_CONV = 3

V7X_LANES = 128
V7X_SUBLANES = 8
V7X_VMEM_BYTES = 64 * 1024 * 1024

MOD_ROWS = 16
MOD_TN = 1536
FFN_TM = 512
FFN_FC = 256
FFN_NCH = D_FF // FFN_FC
PROJ_TM = 512
CONV_TC = 512
CONV_HALO = 16
CONV_RC = 32
NA_QROWS = 2
NA_TQ = NA_QROWS * GRID_W
NA_KROWS = 10
NA_TK = NA_KROWS * GRID_W
NA_NEG = -1e30
ODD_HALO = 8

_BF16 = jnp.bfloat16
_F32 = jnp.float32


def _dot(a, b):
    return jnp.dot(a, b, preferred_element_type=_F32)


def _dot_nt(a, b):
    return lax.dot_general(a, b, (((1,), (1,)), ((), ())), preferred_element_type=_F32)


def _silu(x):
    return x * jax.nn.sigmoid(x)


def _rms(x, g):
    return x * lax.rsqrt(jnp.mean(x * x, axis=-1, keepdims=True) + EPS) * g


def _pre(x, mod_ref, g):
    return _rms(x, g) * (1.0 + mod_ref[0, 1:2, :]) + mod_ref[0, 0:1, :]


def _params(n_axes, vmem_bytes):
    limit = min(int(vmem_bytes), V7X_VMEM_BYTES - 4 * 1024 * 1024)
    return pltpu.CompilerParams(dimension_semantics=("arbitrary",) * n_axes, vmem_limit_bytes=limit)


def _nbytes(shape, dtype):
    return int(np.prod(shape)) * jnp.dtype(dtype).itemsize


def _mod_kernel(cond_ref, w_ref, b_ref, o_ref):
    a = _silu(cond_ref[...]).astype(_BF16)
    o_ref[0] = _dot(a, w_ref[0].astype(_BF16)) + b_ref[0]


def _ada_mod(cond, w_mod, b_mod):
    depth, d, n = w_mod.shape
    vmem = 2 * _nbytes((d, MOD_TN), _F32) + _nbytes((d, MOD_TN), _BF16) + 8 * 1024 * 1024
    return pl.pallas_call(
        _mod_kernel,
        grid=(depth, n // MOD_TN),
        in_specs=[
            pl.BlockSpec((MOD_ROWS, d), lambda i, j: (0, 0)),
            pl.BlockSpec((1, d, MOD_TN), lambda i, j: (i, 0, j)),
            pl.BlockSpec((1, 1, MOD_TN), lambda i, j: (i, 0, j)),
        ],
        out_specs=pl.BlockSpec((1, MOD_ROWS, MOD_TN), lambda i, j: (i, 0, j)),
        out_shape=jax.ShapeDtypeStruct((depth, MOD_ROWS, n), _F32),
        compiler_params=_params(2, vmem),
        name="ada_mod",
    )(cond, w_mod, b_mod.reshape(depth, 1, n))


def _ffn_kernel(x_ref, mod_ref, g_ref, wgu_ref, wd_ref, o_ref, h_ref, acc_ref):
    x = x_ref[0]
    h_ref[...] = _pre(x, mod_ref, g_ref[0:1, :]).astype(_BF16)
    acc_ref[...] = jnp.zeros_like(acc_ref)

    def chunk(c, carry):
        h = h_ref[...]
        gate = _dot(h, wgu_ref[c])
        up = _dot(h, wgu_ref[c + FFN_NCH])
        act = (_silu(gate) * up).astype(_BF16)
        acc_ref[...] += _dot(act, wd_ref[c])
        return carry

    lax.fori_loop(0, FFN_NCH, chunk, 0)
    y = _rms(acc_ref[...], g_ref[1:2, :])
    o_ref[0] = x + FFN_RES * (mod_ref[0, 2:3, :] * y)


def _ffn(x, mod, g, wgu, wd, shared_row=None):
    nb, t, d = x.shape
    tm = min(FFN_TM, t)
    mod_map = (lambda b, i: (b, 0, 0)) if shared_row is None else (lambda b, i: (shared_row, 0, 0))
    vmem = (2 * (_nbytes(wgu.shape, _BF16) + _nbytes(wd.shape, _BF16)) + 4 * _nbytes((tm, d), _F32)
            + _nbytes((tm, d), _BF16) + _nbytes((tm, d), _F32) + 8 * 1024 * 1024)
    return pl.pallas_call(
        _ffn_kernel,
        grid=(nb, t // tm),
        in_specs=[
            pl.BlockSpec((1, tm, d), lambda b, i: (b, i, 0)),
            pl.BlockSpec((1, 3, d), mod_map),
            pl.BlockSpec((2, d), lambda b, i: (0, 0)),
            pl.BlockSpec(wgu.shape, lambda b, i: (0, 0, 0)),
            pl.BlockSpec(wd.shape, lambda b, i: (0, 0, 0)),
        ],
        out_specs=pl.BlockSpec((1, tm, d), lambda b, i: (b, i, 0)),
        out_shape=jax.ShapeDtypeStruct(x.shape, _F32),
        scratch_shapes=[pltpu.VMEM((tm, d), _BF16), pltpu.VMEM((tm, d), _F32)],
        compiler_params=_params(2, vmem),
        name="ffn",
    )(x, mod, g, wgu, wd)


def _prep_ffn(w_gu, w_down):
    d = w_gu.shape[0]
    wgu = w_gu.astype(_BF16).reshape(d, 2 * FFN_NCH, FFN_FC).transpose(1, 0, 2)
    wd = w_down.astype(_BF16).reshape(FFN_NCH, FFN_FC, d)
    return wgu, wd


def _even_in_kernel(x_ref, mod_ref, g_ref, wa_ref, wgt_ref, wq_ref, wkt_ref, wv_ref,
                    glu_ref, q_ref, kt_ref, v_ref):
    h = _pre(x_ref[0], mod_ref, g_ref[...]).astype(_BF16)
    glu_ref[0] = _dot(h, wa_ref[...]) * jax.nn.sigmoid(_dot(h, wgt_ref[...]))
    q_ref[0] = (_dot(h, wq_ref[...]) * (NA_HEAD_DIM ** -0.5)).astype(_BF16)
    v_ref[0] = _dot(h, wv_ref[...]).astype(_BF16)
    kt = _dot_nt(wkt_ref[...], h)
    for j in range(kt.shape[1] // NA_TQ):
        kt_ref[0, j] = kt[:, j * NA_TQ:(j + 1) * NA_TQ].astype(_BF16)


def _even_in(x, mod, g, wa, wgt, wq, wkt, wv):
    nb, t, d = x.shape
    tm = PROJ_TM
    wspec = lambda w: pl.BlockSpec(w.shape, lambda b, i: (0, 0))
    vmem = 2 * 5 * _nbytes((d, CONV_CH), _BF16) + 8 * _nbytes((tm, d), _F32) + 16 * 1024 * 1024
    return pl.pallas_call(
        _even_in_kernel,
        grid=(nb, t // tm),
        in_specs=[
            pl.BlockSpec((1, tm, d), lambda b, i: (b, i, 0)),
            pl.BlockSpec((1, 3, d), lambda b, i: (b, 0, 0)),
            pl.BlockSpec((1, d), lambda b, i: (0, 0)),
            wspec(wa), wspec(wgt), wspec(wq), wspec(wkt), wspec(wv),
        ],
        out_specs=[
            pl.BlockSpec((1, tm, CONV_CH), lambda b, i: (b, i, 0)),
            pl.BlockSpec((1, tm, NA_WIDTH), lambda b, i: (b, i, 0)),
            pl.BlockSpec((1, tm // NA_TQ, NA_WIDTH, NA_TQ), lambda b, i: (b, i, 0, 0)),
            pl.BlockSpec((1, tm, NA_WIDTH), lambda b, i: (b, i, 0)),
        ],
        out_shape=[
            jax.ShapeDtypeStruct((nb, t, CONV_CH), _F32),
            jax.ShapeDtypeStruct((nb, t, NA_WIDTH), _BF16),
            jax.ShapeDtypeStruct((nb, t // NA_TQ, NA_WIDTH, NA_TQ), _BF16),
            jax.ShapeDtypeStruct((nb, t, NA_WIDTH), _BF16),
        ],
        compiler_params=_params(2, vmem),
        name="even_in",
    )(x, mod, g, wa, wgt, wq, wkt, wv)


def _ctx_kv_kernel(x_ref, mod_ref, g_ref, wkt_ref, wv_ref, kt_ref, v_ref):
    h = _pre(x_ref[0], mod_ref, g_ref[...]).astype(_BF16)
    kt_ref[0] = _dot_nt(wkt_ref[...], h).astype(_BF16)
    v_ref[0] = _dot(h, wv_ref[...]).astype(_BF16)


def _ctx_kv(x, mod, g, wkt, wv, shared_row):
    nb, t, d = x.shape
    wspec = lambda w: pl.BlockSpec(w.shape, lambda b: (0, 0))
    return pl.pallas_call(
        _ctx_kv_kernel,
        grid=(nb,),
        in_specs=[
            pl.BlockSpec((1, t, d), lambda b: (b, 0, 0)),
            pl.BlockSpec((1, 3, d), lambda b: (shared_row, 0, 0)),
            pl.BlockSpec((1, d), lambda b: (0, 0)),
            wspec(wkt), wspec(wv),
        ],
        out_specs=[
            pl.BlockSpec((1, NA_WIDTH, t), lambda b: (b, 0, 0)),
            pl.BlockSpec((1, t, NA_WIDTH), lambda b: (b, 0, 0)),
        ],
        out_shape=[
            jax.ShapeDtypeStruct((nb, NA_WIDTH, t), _BF16),
            jax.ShapeDtypeStruct((nb, t, NA_WIDTH), _BF16),
        ],
        compiler_params=_params(1, 24 * 1024 * 1024),
        name="ctx_kv",
    )(x, mod, g, wkt, wv)


def _conv_kernel(main_ref, prev_ref, next_ref, w_ref, b_ref, lng_ref, lnb_ref, grp_ref, o_ref,
                 buf_ref, cv_ref):
    t = pl.program_id(1)
    nt = pl.num_programs(1)
    tc = main_ref.shape[1]
    buf_ref[0:CONV_HALO, :] = jnp.where(t > 0, prev_ref[0], 0.0)
    buf_ref[CONV_HALO:CONV_HALO + tc, :] = main_ref[0]
    buf_ref[CONV_HALO + tc:, :] = jnp.where(t < nt - 1, next_ref[0], 0.0)
    base = CONV_HALO - CONV_WIDTH // 2

    def chunk(r, carry):
        r0 = pl.multiple_of(r * CONV_RC, CONV_RC)
        win = buf_ref[pl.ds(r0, 2 * CONV_RC), :]
        acc = jnp.zeros((CONV_RC, CONV_CH), _F32)
        for k in range(CONV_WIDTH):
            acc = acc + w_ref[k:k + 1, :] * win[base + k:base + k + CONV_RC, :]
        cv_ref[pl.ds(r0, CONV_RC), :] = acc + b_ref[...]
        return carry

    lax.fori_loop(0, tc // CONV_RC, chunk, 0)

    grp = grp_ref[...]
    inv_n = 1.0 / (CONV_CH // CONV_GROUPS)

    def group_mean(z):
        hi = z.astype(_BF16)
        lo = (z - hi.astype(_F32)).astype(_BF16)
        return (_dot(hi, grp) + _dot(lo, grp)) * inv_n

    v = cv_ref[...]
    dlt = v - group_mean(v)
    var = group_mean(dlt * dlt)
    vn = dlt * lax.rsqrt(var + EPS) * lng_ref[...] + lnb_ref[...]
    o_ref[0] = _silu(vn).astype(_BF16)


def _conformer_conv(glu, dw_w, dw_b, ln_g, ln_b):
    nb, t, ch = glu.shape
    tc = CONV_TC
    hb = tc // CONV_HALO
    nh = t // CONV_HALO
    gid = np.arange(ch) // (ch // CONV_GROUPS)
    grp = jnp.asarray(gid[:, None] == gid[None, :], _BF16)
    row = lambda a: a.reshape(1, ch)
    rspec = pl.BlockSpec((1, ch), lambda b, i: (0, 0))
    return pl.pallas_call(
        _conv_kernel,
        grid=(nb, t // tc),
        in_specs=[
            pl.BlockSpec((1, tc, ch), lambda b, i: (b, i, 0)),
            pl.BlockSpec((1, CONV_HALO, ch), lambda b, i: (b, jnp.maximum(i * hb - 1, 0), 0)),
            pl.BlockSpec((1, CONV_HALO, ch), lambda b, i: (b, jnp.minimum((i + 1) * hb, nh - 1), 0)),
            pl.BlockSpec((CONV_WIDTH, ch), lambda b, i: (0, 0)),
            rspec, rspec, rspec,
            pl.BlockSpec((ch, ch), lambda b, i: (0, 0)),
        ],
        out_specs=pl.BlockSpec((1, tc, ch), lambda b, i: (b, i, 0)),
        out_shape=jax.ShapeDtypeStruct((nb, t, ch), _BF16),
        scratch_shapes=[pltpu.VMEM((tc + 2 * CONV_HALO, ch), _F32), pltpu.VMEM((tc, ch), _F32)],
        compiler_params=_params(2, 32 * 1024 * 1024),
        name="conformer_conv",
    )(glu, glu, glu, dw_w, row(dw_b), row(ln_g), row(ln_b), grp)


def _na_bias_table(rpb):
    rows = GRID_W
    pairs = rows // NA_QROWS
    reps = [0, 1, 2, pairs - 2, pairs - 1]
    qr = np.arange(NA_QROWS)[:, None, None, None]
    w = np.arange(GRID_W)[None, :, None, None]
    i = np.arange(NA_KROWS)[None, None, :, None]
    c = np.arange(GRID_W)[None, None, None, :]
    ridx, cidx, ok = [], [], []
    for p in reps:
        kb = np.clip(NA_QROWS * p - NA_WIN_R // 2, 0, rows - NA_KROWS)
        r = NA_QROWS * p + qr
        rs = np.clip(r - NA_WIN_R // 2, 0, rows - NA_WIN_R)
        cs = np.clip(w - NA_WIN_C // 2, 0, GRID_W - NA_WIN_C)
        krow = kb + i
        valid = (krow >= rs) & (krow < rs + NA_WIN_R) & (c >= cs) & (c < cs + NA_WIN_C)
        shape = (NA_QROWS, GRID_W, NA_KROWS, GRID_W)
        ridx.append(np.broadcast_to(np.clip(krow - r + NA_WIN_R - 1, 0, 2 * NA_WIN_R - 2), shape))
        cidx.append(np.broadcast_to(np.clip(c - w + NA_WIN_C - 1, 0, 2 * NA_WIN_C - 2), shape))
        ok.append(np.broadcast_to(valid, shape))
    ridx = np.stack(ridx).reshape(5, NA_TQ, NA_TK)
    cidx = np.stack(cidx).reshape(5, NA_TQ, NA_TK)
    ok = np.stack(ok).reshape(5, NA_TQ, NA_TK)
    gathered = rpb.astype(_F32)[:, ridx, cidx]
    return jnp.where(ok[None], gathered, NA_NEG).transpose(1, 0, 2, 3)


def _na_kernel(q_ref, kt_ref, v_ref, kct_ref, vc_ref, bias_ref, o_ref):
    p = pl.program_id(1)
    n_blocks = kt_ref.shape[1]
    jb = jnp.clip(p - (NA_WIN_R // 2) // NA_QROWS, 0, n_blocks - NA_KROWS // NA_QROWS)
    lane = lax.broadcasted_iota(jnp.int32, (NA_TQ, 2 * NA_HEAD_DIM), 1)
    first_head = lane < NA_HEAD_DIM
    for hp in range(NA_HEADS // 2):
        ls = slice(hp * 2 * NA_HEAD_DIM, (hp + 1) * 2 * NA_HEAD_DIM)
        q2 = q_ref[0, :, ls]
        kt_win = jnp.concatenate([kt_ref[0, jb + j, ls, :] for j in range(NA_KROWS // NA_QROWS)], axis=1)
        v_win = v_ref[0, pl.ds(pl.multiple_of(jb * NA_TQ, NA_TQ), NA_TK), ls]
        kct = kct_ref[0, ls, :]
        vc = vc_ref[0, :, ls]
        outs = []
        for sub in range(2):
            qh = jnp.where(first_head if sub == 0 else jnp.logical_not(first_head), q2, jnp.zeros_like(q2))
            s_w = _dot(qh, kt_win) + bias_ref[0, 2 * hp + sub]
            s_c = _dot(qh, kct)
            m = jnp.maximum(jnp.max(s_w, axis=-1, keepdims=True), jnp.max(s_c, axis=-1, keepdims=True))
            e_w = jnp.exp(s_w - m)
            e_c = jnp.exp(s_c - m)
            denom = jnp.sum(e_w, axis=-1, keepdims=True) + jnp.sum(e_c, axis=-1, keepdims=True)
            o = _dot(e_w.astype(_BF16), v_win) + _dot(e_c.astype(_BF16), vc)
            outs.append(o / denom)
        o_ref[0, :, ls] = jnp.where(first_head, outs[0], outs[1]).astype(_BF16)


def _neighbourhood_attention(q, kt, v, kct, vc, bias):
    nb, t, width = q.shape
    n_blocks = kt.shape[1]
    lc = vc.shape[1]
    pairs = t // NA_TQ

    def bias_type(p):
        return jnp.where(p < 2, p, jnp.where(p >= pairs - 2, p - (pairs - 5), 2))

    vmem = (2 * (2 * _nbytes((t, width), _BF16) + _nbytes(bias.shape[1:], _F32)) + 16 * 1024 * 1024)
    return pl.pallas_call(
        _na_kernel,
        grid=(nb, pairs),
        in_specs=[
            pl.BlockSpec((1, NA_TQ, width), lambda b, p: (b, p, 0)),
            pl.BlockSpec((1, n_blocks, width, NA_TQ), lambda b, p: (b, 0, 0, 0)),
            pl.BlockSpec((1, t, width), lambda b, p: (b, 0, 0)),
            pl.BlockSpec((1, width, lc), lambda b, p: (b, 0, 0)),
            pl.BlockSpec((1, lc, width), lambda b, p: (b, 0, 0)),
            pl.BlockSpec((1,) + bias.shape[1:], lambda b, p: (bias_type(p), 0, 0, 0)),
        ],
        out_specs=pl.BlockSpec((1, NA_TQ, width), lambda b, p: (b, p, 0)),
        out_shape=jax.ShapeDtypeStruct((nb, t, width), _BF16),
        compiler_params=_params(2, vmem),
        name="na_attention",
    )(q, kt, v, kct, vc, bias)


def _even_out_kernel(x_ref, ya_ref, yb_ref, mod_ref, g_ref, woa_ref, wob_ref, o_ref):
    y = _dot(ya_ref[0], woa_ref[...]) + _dot(yb_ref[0], wob_ref[...])
    o_ref[0] = x_ref[0] + mod_ref[0, 2:3, :] * _rms(y, g_ref[...])


def _even_out(x, ya, yb, mod, g, woa, wob):
    nb, t, d = x.shape
    tm = PROJ_TM
    tok = lambda width: pl.BlockSpec((1, tm, width), lambda b, i: (b, i, 0))
    wspec = lambda w: pl.BlockSpec(w.shape, lambda b, i: (0, 0))
    return pl.pallas_call(
        _even_out_kernel,
        grid=(nb, t // tm),
        in_specs=[tok(d), tok(CONV_CH), tok(NA_WIDTH),
                  pl.BlockSpec((1, 3, d), lambda b, i: (b, 0, 0)),
                  pl.BlockSpec((1, d), lambda b, i: (0, 0)),
                  wspec(woa), wspec(wob)],
        out_specs=tok(d),
        out_shape=jax.ShapeDtypeStruct(x.shape, _F32),
        compiler_params=_params(2, 32 * 1024 * 1024),
        name="even_out",
    )(x, ya, yb, mod, g, woa, wob)


def _odd_kernel(x_ref, xp_ref, xn_ref, mod_ref, g_ref, wb_ref, wc_ref, wx_ref, cw_ref, wo_ref, o_ref):
    t = pl.program_id(1)
    nt = pl.num_programs(1)
    x = x_ref[0]
    tm = x.shape[0]
    n = tm + 2 * ODD_HALO
    xa = jnp.concatenate([xp_ref[0], x, xn_ref[0]], axis=0)
    ha = _pre(xa, mod_ref, g_ref[0:1, :]).astype(_BF16)
    cz = _dot(ha, wc_ref[...]) * _dot(ha, wx_ref[...])
    row = lax.broadcasted_iota(jnp.int32, (n, 1), 0)
    inside = jnp.logical_and(jnp.logical_or(row >= ODD_HALO, t > 0),
                             jnp.logical_or(row < tm + ODD_HALO, t < nt - 1))
    cz = jnp.where(inside, cz, 0.0)
    mid = slice(ODD_HALO, ODD_HALO + tm)
    y = (cw_ref[0:1, :] * pltpu.roll(cz, 1, 0)[mid]
         + cw_ref[1:2, :] * cz[mid]
         + cw_ref[2:3, :] * pltpu.roll(cz, n - 1, 0)[mid])
    z = (_dot(ha[mid], wb_ref[...]) * y).astype(_BF16)
    out = _dot(z, wo_ref[...])
    o_ref[0] = x + mod_ref[0, 2:3, :] * _rms(out, g_ref[1:2, :])


def _odd_mixer(x, mod, g, wb, wc, wx, cw, wo):
    nb, t, d = x.shape
    tm = PROJ_TM
    hb = tm // ODD_HALO
    nh = t // ODD_HALO
    wspec = lambda w: pl.BlockSpec(w.shape, lambda b, i: (0, 0))
    vmem = 2 * 4 * _nbytes((d, d), _BF16) + 12 * _nbytes((tm + 2 * ODD_HALO, d), _F32) + 8 * 1024 * 1024
    return pl.pallas_call(
        _odd_kernel,
        grid=(nb, t // tm),
        in_specs=[
            pl.BlockSpec((1, tm, d), lambda b, i: (b, i, 0)),
            pl.BlockSpec((1, ODD_HALO, d), lambda b, i: (b, jnp.maximum(i * hb - 1, 0), 0)),
            pl.BlockSpec((1, ODD_HALO, d), lambda b, i: (b, jnp.minimum((i + 1) * hb, nh - 1), 0)),
            pl.BlockSpec((1, 3, d), lambda b, i: (b, 0, 0)),
            pl.BlockSpec((2, d), lambda b, i: (0, 0)),
            wspec(wb), wspec(wc), wspec(wx), wspec(cw), wspec(wo),
        ],
        out_specs=pl.BlockSpec((1, tm, d), lambda b, i: (b, i, 0)),
        out_shape=jax.ShapeDtypeStruct(x.shape, _F32),
        compiler_params=_params(2, vmem),
        name="odd_mixer",
    )(x, x, x, mod, g, wb, wc, wx, cw, wo)


def kernel(x, c, ctx, c_ctx, w_mod, b_mod, norm_g, ff1_w_gu, ff1_w_down, ff2_w_gu, ff2_w_down,
           ev_w_in, ev_w_out, ev_dw_w, ev_dw_b, ev_ln_g, ev_ln_b, ev_rpb, od_w_in, od_conv_w, od_w_out):
    bsz, seq, d = x.shape
    lc = ctx.shape[1]
    ctx_row = bsz
    cond = jnp.concatenate([c, c_ctx[None, :], jnp.zeros((MOD_ROWS - bsz - 1, d), _F32)], axis=0)
    mods = _ada_mod(cond, w_mod, b_mod).reshape(w_mod.shape[0], MOD_ROWS, N_MOD, d)

    m, g = mods[0], norm_g[0]
    wgu, wd = _prep_ffn(ff1_w_gu[0], ff1_w_down[0])
    x = _ffn(x, m[:, 0:3], g[0:2], wgu, wd)
    x_ctx = _ffn(ctx.reshape(1, bsz * lc, d), m[:, 0:3], g[0:2], wgu, wd, shared_row=ctx_row)
    x_ctx = x_ctx.reshape(bsz, lc, d)

    w_in = ev_w_in[0].astype(_BF16)
    wa, wgt = w_in[:, :CONV_CH], w_in[:, CONV_CH:2 * CONV_CH]
    wq, wk, wv = jnp.split(w_in[:, 2 * CONV_CH:], 3, axis=1)
    wkt = wk.T
    glu, q, kt, v = _even_in(x, m[:, 3:6], g[2:3], wa, wgt, wq, wkt, wv)
    kct, vc = _ctx_kv(x_ctx, m[:, 3:6], g[2:3], wkt, wv, shared_row=ctx_row)
    ya = _conformer_conv(glu, ev_dw_w[0], ev_dw_b[0], ev_ln_g[0], ev_ln_b[0])
    yb = _neighbourhood_attention(q, kt, v, kct, vc, _na_bias_table(ev_rpb[0]))
    w_out = ev_w_out[0].astype(_BF16)
    x = _even_out(x, ya, yb, m[:, 3:6], g[3:4], w_out[:CONV_CH], w_out[CONV_CH:])

    wgu, wd = _prep_ffn(ff2_w_gu[0], ff2_w_down[0])
    x = _ffn(x, m[:, 6:9], g[4:6], wgu, wd)

    m, g = mods[1], norm_g[1]
    wgu, wd = _prep_ffn(ff1_w_gu[1], ff1_w_down[1])
    x = _ffn(x, m[:, 0:3], g[0:2], wgu, wd)
    wb, wc, wx = jnp.split(od_w_in[0].astype(_BF16), 3, axis=1)
    x = _odd_mixer(x, m[:, 3:6], g[2:4], wb, wc, wx, od_conv_w[0], od_w_out[0].astype(_BF16))
    wgu, wd = _prep_ffn(ff2_w_gu[1], ff2_w_down[1])
    x = _ffn(x, m[:, 6:9], g[4:6], wgu, wd)
    return x
```

```python
import functools

import jax
import jax.numpy as jnp
import numpy as np
from jax import lax
from jax.experimental import pallas as pl
from jax.experimental.pallas import tpu as pltpu

D_MODEL = 1024
D_FF = 2816
N_MOD = 9
EPS = 1e-6
FFN_RES = 0.5
GRID_W = 64
CONV_CH = 512
CONV_GROUPS = 8
CONV_WIDTH = 31
NA_HEADS = 8
NA_HEAD_DIM = 64
NA_WIDTH = NA_HEADS * NA_HEAD_DIM
NA_WIN_R = 8
NA_WIN_C = 16
SC_CONV = 3

V7X_LANES = 128
V7X_SUBLANES = 8
V7X_VMEM_BYTES = 64 * 1024 * 1024

MOD_ROWS = 16
MOD_TN = 1536
FFN_TM = 1024
FFN_SUB = 512
FFN_FC = 256
FFN_NCH = D_FF // FFN_FC
PROJ_TM = 512
CONV_TC = 512
CONV_HALO = 16
CONV_RC = 32
NA_QROWS = 2
NA_TQ = NA_QROWS * GRID_W
NA_KROWS = 10
NA_TK = NA_KROWS * GRID_W
NA_NEG = -1e30
NA_VTILE = 2 * NA_HEAD_DIM
ODD_HALO = 8

_BF16 = jnp.bfloat16
_F32 = jnp.float32


def _dot(a, b):
    return jnp.dot(a, b, preferred_element_type=_F32)


def _dot_nt(a, b):
    return lax.dot_general(a, b, (((1,), (1,)), ((), ())), preferred_element_type=_F32)


def _silu(x):
    return x * jax.nn.sigmoid(x)


def _rms(x, row):
    return x * lax.rsqrt(jnp.mean(x * x, axis=-1, keepdims=True) + EPS) * row


def _pre(x, mod_ref, g):
    return _rms(x, g * (1.0 + mod_ref[0, 1:2, :])) + mod_ref[0, 0:1, :]


def _post(y, mod_ref, g, weight=1.0):
    return _rms(y, (weight * mod_ref[0, 2:3, :]) * g)


def _params(n_axes, vmem_bytes):
    limit = min(int(vmem_bytes), V7X_VMEM_BYTES - 4 * 1024 * 1024)
    return pltpu.CompilerParams(dimension_semantics=("arbitrary",) * n_axes, vmem_limit_bytes=limit)


def _nbytes(shape, dtype):
    return int(np.prod(shape)) * jnp.dtype(dtype).itemsize


def _mod_kernel(cond_ref, w_ref, b_ref, o_ref):
    a = _silu(cond_ref[...]).astype(_BF16)
    o_ref[0] = _dot(a, w_ref[0].astype(_BF16)) + b_ref[0]


def _ada_mod(cond, w_mod, b_mod):
    depth, d, n = w_mod.shape
    vmem = 2 * _nbytes((d, MOD_TN), _F32) + _nbytes((d, MOD_TN), _BF16) + 8 * 1024 * 1024
    return pl.pallas_call(
        _mod_kernel,
        grid=(depth, n // MOD_TN),
        in_specs=[
            pl.BlockSpec((MOD_ROWS, d), lambda i, j: (0, 0)),
            pl.BlockSpec((1, d, MOD_TN), lambda i, j: (i, 0, j)),
            pl.BlockSpec((1, 1, MOD_TN), lambda i, j: (i, 0, j)),
        ],
        out_specs=pl.BlockSpec((1, MOD_ROWS, MOD_TN), lambda i, j: (i, 0, j)),
        out_shape=jax.ShapeDtypeStruct((depth, MOD_ROWS, n), _F32),
        compiler_params=_params(2, vmem),
        name="ada_mod",
    )(cond, w_mod, b_mod.reshape(depth, 1, n))


def _ffn_kernel(x_ref, mod_ref, g_ref, wgu_ref, wd_ref, o_ref, h_ref, acc_ref):
    tm = x_ref.shape[1]
    sub = min(FFN_SUB, tm)
    subs = [slice(s * sub, (s + 1) * sub) for s in range(tm // sub)]
    for rows in subs:
        h_ref[rows, :] = _pre(x_ref[0, rows, :], mod_ref, g_ref[0:1, :]).astype(_BF16)
    for rows in subs:
        for c in range(FFN_NCH):
            h = h_ref[rows, :]
            gate = _dot(h, wgu_ref[:, c * FFN_FC:(c + 1) * FFN_FC])
            up = _dot(h, wgu_ref[:, D_FF + c * FFN_FC:D_FF + (c + 1) * FFN_FC])
            act = (_silu(gate) * up).astype(_BF16)
            part = _dot(act, wd_ref[c * FFN_FC:(c + 1) * FFN_FC, :])
            if c == 0:
                acc_ref[rows, :] = part
            else:
                acc_ref[rows, :] += part
    for rows in subs:
        o_ref[0, rows, :] = x_ref[0, rows, :] + _post(acc_ref[rows, :], mod_ref, g_ref[1:2, :], FFN_RES)


def _ffn(x, mod, g, wgu, wd, shared_row=None):
    nb, t, d = x.shape
    tm = min(FFN_TM, t)
    mod_map = (lambda b, i: (b, 0, 0)) if shared_row is None else (lambda b, i: (shared_row, 0, 0))
    resident = pl.Buffered(1)
    vmem = (_nbytes(wgu.shape, _BF16) + _nbytes(wd.shape, _BF16) + 4 * _nbytes((tm, d), _F32)
            + _nbytes((tm, d), _BF16) + _nbytes((tm, d), _F32) + 8 * 1024 * 1024)
    return pl.pallas_call(
        _ffn_kernel,
        grid=(nb, t // tm),
        in_specs=[
            pl.BlockSpec((1, tm, d), lambda b, i: (b, i, 0)),
            pl.BlockSpec((1, 3, d), mod_map),
            pl.BlockSpec((2, d), lambda b, i: (0, 0)),
            pl.BlockSpec(wgu.shape, lambda b, i: (0, 0), pipeline_mode=resident),
            pl.BlockSpec(wd.shape, lambda b, i: (0, 0), pipeline_mode=resident),
        ],
        out_specs=pl.BlockSpec((1, tm, d), lambda b, i: (b, i, 0)),
        out_shape=jax.ShapeDtypeStruct(x.shape, _F32),
        scratch_shapes=[pltpu.VMEM((tm, d), _BF16), pltpu.VMEM((tm, d), _F32)],
        compiler_params=_params(2, vmem),
        name="ffn",
    )(x, mod, g, wgu, wd)


def _prep_ffn(w_gu, w_down):
    return w_gu.astype(_BF16), w_down.astype(_BF16)


def _store_values_with_ones(v, v_ref):
    lane = lax.broadcasted_iota(jnp.int32, (v.shape[0], NA_VTILE), 1)
    first_head = lane < NA_HEAD_DIM
    for j in range(NA_HEADS // 2):
        pair = v[:, j * NA_VTILE:(j + 1) * NA_VTILE]
        v_ref[0, :, 2 * j * NA_VTILE:(2 * j + 1) * NA_VTILE] = jnp.where(first_head, pair, 1.0).astype(_BF16)
        v_ref[0, :, (2 * j + 1) * NA_VTILE:(2 * j + 2) * NA_VTILE] = jnp.where(first_head, 1.0, pair).astype(_BF16)


def _even_in_kernel(x_ref, mod_ref, g_ref, wa_ref, wgt_ref, wq_ref, wkt_ref, wv_ref,
                    glu_ref, q_ref, kt_ref, v_ref):
    h = _pre(x_ref[0], mod_ref, g_ref[...]).astype(_BF16)
    glu_ref[0] = _dot(h, wa_ref[...]) * jax.nn.sigmoid(_dot(h, wgt_ref[...]))
    q_ref[0] = (_dot(h, wq_ref[...]) * (NA_HEAD_DIM ** -0.5)).astype(_BF16)
    _store_values_with_ones(_dot(h, wv_ref[...]), v_ref)
    kt = _dot_nt(wkt_ref[...], h)
    for j in range(kt.shape[1] // NA_TQ):
        kt_ref[0, j] = kt[:, j * NA_TQ:(j + 1) * NA_TQ].astype(_BF16)


def _even_in(x, mod, g, wa, wgt, wq, wkt, wv):
    nb, t, d = x.shape
    tm = PROJ_TM
    wspec = lambda w: pl.BlockSpec(w.shape, lambda b, i: (0, 0))
    vmem = 2 * 5 * _nbytes((d, CONV_CH), _BF16) + 8 * _nbytes((tm, d), _F32) + 16 * 1024 * 1024
    return pl.pallas_call(
        _even_in_kernel,
        grid=(nb, t // tm),
        in_specs=[
            pl.BlockSpec((1, tm, d), lambda b, i: (b, i, 0)),
            pl.BlockSpec((1, 3, d), lambda b, i: (b, 0, 0)),
            pl.BlockSpec((1, d), lambda b, i: (0, 0)),
            wspec(wa), wspec(wgt), wspec(wq), wspec(wkt), wspec(wv),
        ],
        out_specs=[
            pl.BlockSpec((1, tm, CONV_CH), lambda b, i: (b, i, 0)),
            pl.BlockSpec((1, tm, NA_WIDTH), lambda b, i: (b, i, 0)),
            pl.BlockSpec((1, tm // NA_TQ, NA_WIDTH, NA_TQ), lambda b, i: (b, i, 0, 0)),
            pl.BlockSpec((1, tm, NA_HEADS * NA_VTILE), lambda b, i: (b, i, 0)),
        ],
        out_shape=[
            jax.ShapeDtypeStruct((nb, t, CONV_CH), _F32),
            jax.ShapeDtypeStruct((nb, t, NA_WIDTH), _BF16),
            jax.ShapeDtypeStruct((nb, t // NA_TQ, NA_WIDTH, NA_TQ), _BF16),
            jax.ShapeDtypeStruct((nb, t, NA_HEADS * NA_VTILE), _BF16),
        ],
        compiler_params=_params(2, vmem),
        name="even_in",
    )(x, mod, g, wa, wgt, wq, wkt, wv)


def _ctx_kv_kernel(x_ref, mod_ref, g_ref, wkt_ref, wv_ref, kt_ref, v_ref):
    h = _pre(x_ref[0], mod_ref, g_ref[...]).astype(_BF16)
    kt_ref[0] = _dot_nt(wkt_ref[...], h).astype(_BF16)
    _store_values_with_ones(_dot(h, wv_ref[...]), v_ref)


def _ctx_kv(x, mod, g, wkt, wv, shared_row):
    nb, t, d = x.shape
    wspec = lambda w: pl.BlockSpec(w.shape, lambda b: (0, 0))
    return pl.pallas_call(
        _ctx_kv_kernel,
        grid=(nb,),
        in_specs=[
            pl.BlockSpec((1, t, d), lambda b: (b, 0, 0)),
            pl.BlockSpec((1, 3, d), lambda b: (shared_row, 0, 0)),
            pl.BlockSpec((1, d), lambda b: (0, 0)),
            wspec(wkt), wspec(wv),
        ],
        out_specs=[
            pl.BlockSpec((1, NA_WIDTH, t), lambda b: (b, 0, 0)),
            pl.BlockSpec((1, t, NA_HEADS * NA_VTILE), lambda b: (b, 0, 0)),
        ],
        out_shape=[
            jax.ShapeDtypeStruct((nb, NA_WIDTH, t), _BF16),
            jax.ShapeDtypeStruct((nb, t, NA_HEADS * NA_VTILE), _BF16),
        ],
        compiler_params=_params(1, 24 * 1024 * 1024),
        name="ctx_kv",
    )(x, mod, g, wkt, wv)


def _conv_kernel(main_ref, prev_ref, next_ref, w_ref, b_ref, lng_ref, lnb_ref, grp_ref, o_ref,
                 buf_ref, cv_ref):
    t = pl.program_id(1)
    nt = pl.num_programs(1)
    tc = main_ref.shape[1]
    buf_ref[0:CONV_HALO, :] = jnp.where(t > 0, prev_ref[0], 0.0)
    buf_ref[CONV_HALO:CONV_HALO + tc, :] = main_ref[0]
    buf_ref[CONV_HALO + tc:, :] = jnp.where(t < nt - 1, next_ref[0], 0.0)
    base = CONV_HALO - CONV_WIDTH // 2

    def chunk(r, carry):
        r0 = pl.multiple_of(r * CONV_RC, CONV_RC)
        acc = None
        for s in range(V7X_SUBLANES):
            part = None
            for a in range((base + CONV_WIDTH - 1) // V7X_SUBLANES + 1):
                k = V7X_SUBLANES * a + s - base
                if 0 <= k < CONV_WIDTH:
                    win = buf_ref[pl.ds(r0 + V7X_SUBLANES * a, CONV_RC + V7X_SUBLANES), :]
                    term = w_ref[k:k + 1, :] * win
                    part = term if part is None else part + term
            shifted = part[s:s + CONV_RC, :]
            acc = shifted if acc is None else acc + shifted
        cv_ref[pl.ds(r0, CONV_RC), :] = acc + b_ref[...]
        return carry

    lax.fori_loop(0, tc // CONV_RC, chunk, 0)

    grp = grp_ref[...]
    inv_n = 1.0 / (CONV_CH // CONV_GROUPS)

    def group_mean(z):
        hi = z.astype(_BF16)
        lo = (z - hi.astype(_F32)).astype(_BF16)
        return (_dot(hi, grp) + _dot(lo, grp)) * inv_n

    v = cv_ref[...]
    dlt = v - group_mean(v)
    var = group_mean(dlt * dlt)
    vn = dlt * lax.rsqrt(var + EPS) * lng_ref[...] + lnb_ref[...]
    o_ref[0] = _silu(vn).astype(_BF16)


def _conformer_conv(glu, dw_w, dw_b, ln_g, ln_b):
    nb, t, ch = glu.shape
    tc = CONV_TC
    hb = tc // CONV_HALO
    nh = t // CONV_HALO
    gid = np.arange(ch) // (ch // CONV_GROUPS)
    grp = jnp.asarray(gid[:, None] == gid[None, :], _BF16)
    row = lambda a: a.reshape(1, ch)
    rspec = pl.BlockSpec((1, ch), lambda b, i: (0, 0))
    return pl.pallas_call(
        _conv_kernel,
        grid=(nb, t // tc),
        in_specs=[
            pl.BlockSpec((1, tc, ch), lambda b, i: (b, i, 0)),
            pl.BlockSpec((1, CONV_HALO, ch), lambda b, i: (b, jnp.maximum(i * hb - 1, 0), 0)),
            pl.BlockSpec((1, CONV_HALO, ch), lambda b, i: (b, jnp.minimum((i + 1) * hb, nh - 1), 0)),
            pl.BlockSpec((CONV_WIDTH, ch), lambda b, i: (0, 0)),
            rspec, rspec, rspec,
            pl.BlockSpec((ch, ch), lambda b, i: (0, 0)),
        ],
        out_specs=pl.BlockSpec((1, tc, ch), lambda b, i: (b, i, 0)),
        out_shape=jax.ShapeDtypeStruct((nb, t, ch), _BF16),
        scratch_shapes=[pltpu.VMEM((tc + 2 * CONV_HALO, ch), _F32), pltpu.VMEM((tc, ch), _F32)],
        compiler_params=_params(2, 32 * 1024 * 1024),
        name="conformer_conv",
    )(glu, glu, glu, dw_w, row(dw_b), row(ln_g), row(ln_b), grp)


def _na_bias_table(rpb):
    rows = GRID_W
    pairs = rows // NA_QROWS
    reps = [0, 1, 2, pairs - 2, pairs - 1]
    heads, n_rr, n_cc = rpb.shape
    span = 2 * GRID_W
    left = GRID_W - NA_WIN_C
    g = jnp.pad(rpb.astype(_F32), ((0, 0), (0, 0), (left, span - left - n_cc)))
    skew = jnp.tile(g, (1, 1, GRID_W))[..., :GRID_W * (span - 1)].reshape(heads, n_rr, GRID_W, span - 1)
    toe = skew[..., GRID_W - 1:2 * GRID_W - 1]
    qr = np.arange(NA_QROWS)[:, None, None, None]
    w = np.arange(GRID_W)[None, :, None, None]
    i = np.arange(NA_KROWS)[None, None, :, None]
    c = np.arange(GRID_W)[None, None, None, :]
    rr_sel, ok = [], []
    for p in reps:
        kb = np.clip(NA_QROWS * p - NA_WIN_R // 2, 0, rows - NA_KROWS)
        r = NA_QROWS * p + qr
        rs = np.clip(r - NA_WIN_R // 2, 0, rows - NA_WIN_R)
        cs = np.clip(w - NA_WIN_C // 2, 0, GRID_W - NA_WIN_C)
        krow = kb + i
        valid = (krow >= rs) & (krow < rs + NA_WIN_R) & (c >= cs) & (c < cs + NA_WIN_C)
        ok.append(np.broadcast_to(valid, (NA_QROWS, GRID_W, NA_KROWS, GRID_W)))
        rr_sel.append(np.clip(krow - r + NA_WIN_R - 1, 0, n_rr - 1)[:, 0, :, 0])
    ok = np.stack(ok).reshape(len(reps), 1, NA_TQ, NA_TK)
    rr_flat = np.stack(rr_sel).reshape(-1)
    sel = jnp.stack([toe[:, int(rr)] for rr in rr_flat], axis=1)
    sel = sel.reshape(heads, len(reps), NA_QROWS, NA_KROWS, GRID_W, GRID_W)
    sel = sel.transpose(1, 0, 2, 4, 3, 5).reshape(len(reps), heads, NA_TQ, NA_TK)
    return jnp.where(ok, sel, NA_NEG)


def _na_kernel(q_ref, kt_ref, v_ref, kct_ref, vc_ref, bias_ref, o_ref):
    p = pl.program_id(1)
    n_blocks = kt_ref.shape[1]
    jb = jnp.clip(p - (NA_WIN_R // 2) // NA_QROWS, 0, n_blocks - NA_KROWS // NA_QROWS)
    lane = lax.broadcasted_iota(jnp.int32, (NA_TQ, 2 * NA_HEAD_DIM), 1)
    first_head = lane < NA_HEAD_DIM
    win_rows = pl.ds(pl.multiple_of(jb * NA_TQ, NA_TQ), NA_TK)

    def scores(h):
        ls = slice((h // 2) * 2 * NA_HEAD_DIM, (h // 2 + 1) * 2 * NA_HEAD_DIM)
        q2 = q_ref[0, :, ls]
        qh = jnp.where(first_head if h % 2 == 0 else jnp.logical_not(first_head), q2, jnp.zeros_like(q2))
        kt_win = jnp.concatenate([kt_ref[0, jb + j, ls, :] for j in range(NA_KROWS // NA_QROWS)], axis=1)
        return _dot(qh, kt_win) + bias_ref[0, h], _dot(qh, kct_ref[0, ls, :])

    pending = scores(0)
    tiles = []
    for h in range(NA_HEADS):
        s_w, s_c = pending
        if h + 1 < NA_HEADS:
            pending = scores(h + 1)
        m = jnp.maximum(jnp.max(s_w, axis=-1, keepdims=True), jnp.max(s_c, axis=-1, keepdims=True))
        e_w = jnp.exp(s_w - m).astype(_BF16)
        e_c = jnp.exp(s_c - m).astype(_BF16)
        hl = slice(h * NA_VTILE, (h + 1) * NA_VTILE)
        tiles.append(_dot(e_w, v_ref[0, win_rows, hl]) + _dot(e_c, vc_ref[0, :, hl]))
        if h % 2 == 1:
            num = jnp.where(first_head, tiles[h - 1], tiles[h])
            den = jnp.where(first_head, pltpu.roll(tiles[h - 1], NA_HEAD_DIM, 1),
                            pltpu.roll(tiles[h], NA_HEAD_DIM, 1))
            ls = slice((h // 2) * 2 * NA_HEAD_DIM, (h // 2 + 1) * 2 * NA_HEAD_DIM)
            o_ref[0, :, ls] = (num / den).astype(_BF16)


def _neighbourhood_attention(q, kt, v, kct, vc, bias):
    nb, t, width = q.shape
    n_blocks = kt.shape[1]
    lc = vc.shape[1]
    pairs = t // NA_TQ

    def bias_type(p):
        return jnp.where(p < 2, p, jnp.where(p >= pairs - 2, p - (pairs - 5), 2))

    vmem = (2 * (_nbytes((t, width), _BF16) + _nbytes(v.shape[1:], _BF16) + _nbytes(bias.shape[1:], _F32))
            + 16 * 1024 * 1024)
    return pl.pallas_call(
        _na_kernel,
        grid=(nb, pairs),
        in_specs=[
            pl.BlockSpec((1, NA_TQ, width), lambda b, p: (b, p, 0)),
            pl.BlockSpec((1, n_blocks, width, NA_TQ), lambda b, p: (b, 0, 0, 0)),
            pl.BlockSpec((1,) + v.shape[1:], lambda b, p: (b, 0, 0)),
            pl.BlockSpec((1, width, lc), lambda b, p: (b, 0, 0)),
            pl.BlockSpec((1,) + vc.shape[1:], lambda b, p: (b, 0, 0)),
            pl.BlockSpec((1,) + bias.shape[1:], lambda b, p: (bias_type(p), 0, 0, 0)),
        ],
        out_specs=pl.BlockSpec((1, NA_TQ, width), lambda b, p: (b, p, 0)),
        out_shape=jax.ShapeDtypeStruct((nb, t, width), _BF16),
        compiler_params=_params(2, vmem),
        name="na_attention",
    )(q, kt, v, kct, vc, bias)


def _even_out_kernel(x_ref, ya_ref, yb_ref, mod_ref, g_ref, woa_ref, wob_ref, o_ref):
    y = _dot(ya_ref[0], woa_ref[...]) + _dot(yb_ref[0], wob_ref[...])
    o_ref[0] = x_ref[0] + _post(y, mod_ref, g_ref[...])


def _even_out(x, ya, yb, mod, g, woa, wob):
    nb, t, d = x.shape
    tm = PROJ_TM
    tok = lambda width: pl.BlockSpec((1, tm, width), lambda b, i: (b, i, 0))
    wspec = lambda w: pl.BlockSpec(w.shape, lambda b, i: (0, 0))
    return pl.pallas_call(
        _even_out_kernel,
        grid=(nb, t // tm),
        in_specs=[tok(d), tok(CONV_CH), tok(NA_WIDTH),
                  pl.BlockSpec((1, 3, d), lambda b, i: (b, 0, 0)),
                  pl.BlockSpec((1, d), lambda b, i: (0, 0)),
                  wspec(woa), wspec(wob)],
        out_specs=tok(d),
        out_shape=jax.ShapeDtypeStruct(x.shape, _F32),
        compiler_params=_params(2, 32 * 1024 * 1024),
        name="even_out",
    )(x, ya, yb, mod, g, woa, wob)


def _odd_kernel(x_ref, xp_ref, xn_ref, mod_ref, g_ref, wb_ref, wc_ref, wx_ref, cw_ref, wo_ref, o_ref):
    t = pl.program_id(1)
    nt = pl.num_programs(1)
    x = x_ref[0]
    tm = x.shape[0]
    n = tm + 2 * ODD_HALO
    xa = jnp.concatenate([xp_ref[0], x, xn_ref[0]], axis=0)
    ha = _pre(xa, mod_ref, g_ref[0:1, :]).astype(_BF16)
    cz = _dot(ha, wc_ref[...]) * _dot(ha, wx_ref[...])
    row = lax.broadcasted_iota(jnp.int32, (n, 1), 0)
    inside = jnp.logical_and(jnp.logical_or(row >= ODD_HALO, t > 0),
                             jnp.logical_or(row < tm + ODD_HALO, t < nt - 1))
    cz = jnp.where(inside, cz, 0.0)
    mid = slice(ODD_HALO, ODD_HALO + tm)
    y = (cw_ref[0:1, :] * pltpu.roll(cz, 1, 0)[mid]
         + cw_ref[1:2, :] * cz[mid]
         + cw_ref[2:3, :] * pltpu.roll(cz, n - 1, 0)[mid])
    z = (_dot(ha[mid], wb_ref[...]) * y).astype(_BF16)
    out = _dot(z, wo_ref[...])
    o_ref[0] = x + _post(out, mod_ref, g_ref[1:2, :])


def _odd_mixer(x, mod, g, wb, wc, wx, cw, wo):
    nb, t, d = x.shape
    tm = PROJ_TM
    hb = tm // ODD_HALO
    nh = t // ODD_HALO
    wspec = lambda w: pl.BlockSpec(w.shape, lambda b, i: (0, 0))
    vmem = 2 * 4 * _nbytes((d, d), _BF16) + 12 * _nbytes((tm + 2 * ODD_HALO, d), _F32) + 8 * 1024 * 1024
    return pl.pallas_call(
        _odd_kernel,
        grid=(nb, t // tm),
        in_specs=[
            pl.BlockSpec((1, tm, d), lambda b, i: (b, i, 0)),
            pl.BlockSpec((1, ODD_HALO, d), lambda b, i: (b, jnp.maximum(i * hb - 1, 0), 0)),
            pl.BlockSpec((1, ODD_HALO, d), lambda b, i: (b, jnp.minimum((i + 1) * hb, nh - 1), 0)),
            pl.BlockSpec((1, 3, d), lambda b, i: (b, 0, 0)),
            pl.BlockSpec((2, d), lambda b, i: (0, 0)),
            wspec(wb), wspec(wc), wspec(wx), wspec(cw), wspec(wo),
        ],
        out_specs=pl.BlockSpec((1, tm, d), lambda b, i: (b, i, 0)),
        out_shape=jax.ShapeDtypeStruct(x.shape, _F32),
        compiler_params=_params(2, vmem),
        name="odd_mixer",
    )(x, x, x, mod, g, wb, wc, wx, cw, wo)


def kernel(x, c, ctx, c_ctx, w_mod, b_mod, norm_g, ff1_w_gu, ff1_w_down, ff2_w_gu, ff2_w_down,
           ev_w_in, ev_w_out, ev_dw_w, ev_dw_b, ev_ln_g, ev_ln_b, ev_rpb, od_w_in, od_conv_w, od_w_out):
    bsz, seq, d = x.shape
    lc = ctx.shape[1]
    ctx_row = bsz
    cond = jnp.concatenate([c, c_ctx[None, :], jnp.zeros((MOD_ROWS - bsz - 1, d), _F32)], axis=0)
    mods = _ada_mod(cond, w_mod, b_mod).reshape(w_mod.shape[0], MOD_ROWS, N_MOD, d)

    m, g = mods[0], norm_g[0]
    wgu, wd = _prep_ffn(ff1_w_gu[0], ff1_w_down[0])
    x = _ffn(x, m[:, 0:3], g[0:2], wgu, wd)
    x_ctx = _ffn(ctx.reshape(1, bsz * lc, d), m[:, 0:3], g[0:2], wgu, wd, shared_row=ctx_row)
    x_ctx = x_ctx.reshape(bsz, lc, d)

    w_in = ev_w_in[0].astype(_BF16)
    wa, wgt = w_in[:, :CONV_CH], w_in[:, CONV_CH:2 * CONV_CH]
    wq, wk, wv = jnp.split(w_in[:, 2 * CONV_CH:], 3, axis=1)
    wkt = wk.T
    glu, q, kt, v = _even_in(x, m[:, 3:6], g[2:3], wa, wgt, wq, wkt, wv)
    kct, vc = _ctx_kv(x_ctx, m[:, 3:6], g[2:3], wkt, wv, shared_row=ctx_row)
    ya = _conformer_conv(glu, ev_dw_w[0], ev_dw_b[0], ev_ln_g[0], ev_ln_b[0])
    yb = _neighbourhood_attention(q, kt, v, kct, vc, _na_bias_table(ev_rpb[0]))
    w_out = ev_w_out[0].astype(_BF16)
    x = _even_out(x, ya, yb, m[:, 3:6], g[3:4], w_out[:CONV_CH], w_out[CONV_CH:])

    wgu, wd = _prep_ffn(ff2_w_gu[0], ff2_w_down[0])
    x = _ffn(x, m[:, 6:9], g[4:6], wgu, wd)

    m, g = mods[1], norm_g[1]
    wgu, wd = _prep_ffn(ff1_w_gu[1], ff1_w_down[1])
    x = _ffn(x, m[:, 0:3], g[0:2], wgu, wd)
    wb, wc, wx = jnp.split(od_w_in[0].astype(_BF16), 3, axis=1)
    x = _odd_mixer(x, m[:, 3:6], g[2:4], wb, wc, wx, od_conv_w[0], od_w_out[0].astype(_BF16))
    wgu, wd = _prep_ffn(ff2_w_gu[1], ff2_w_down[1])
    x = _ffn(x, m[:, 6:9], g[4:6], wgu, wd)
    return x
```

```python
import functools

import jax
import jax.numpy as jnp
import numpy as np
from jax import lax
from jax.experimental import pallas as pl
from jax.experimental.pallas import tpu as pltpu

D_MODEL = 1024
D_FF = 2816
N_MOD = 9
EPS = 1e-6
FFN_RES = 0.5
GRID_W = 64
CONV_CH = 512
CONV_GROUPS = 8
CONV_WIDTH = 31
NA_HEADS = 8
NA_HEAD_DIM = 64
NA_WIDTH = NA_HEADS * NA_HEAD_DIM
NA_WIN_R = 8
NA_WIN_C = 16
SC_CONV = 3

V7X_LANES = 128
V7X_SUBLANES = 8
V7X_VMEM_BYTES = 64 * 1024 * 1024
BF16_SUBLANES = 16

MOD_ROWS = 16
MOD_TN = 1536
CAST_BLOCK_BYTES = 6 * 1024 * 1024
FFN_TM = 1024
FFN_SUB = 512
FFN_FC = 256
FFN_NCH = D_FF // FFN_FC
PROJ_TM = 512
CONV_TC = 512
CONV_HALO = 16
CONV_RC = 32
NA_QROWS = 2
NA_TQ = NA_QROWS * GRID_W
NA_KROWS = 10
NA_TK = NA_KROWS * GRID_W
NA_NEG = -1e30
NA_VTILE = 2 * NA_HEAD_DIM
NA_STEP_PAIRS = 2
ODD_HALO = 8

_BF16 = jnp.bfloat16
_F32 = jnp.float32


def _dot(a, b):
    return jnp.dot(a, b, preferred_element_type=_F32)


def _dot_nt(a, b):
    return lax.dot_general(a, b, (((1,), (1,)), ((), ())), preferred_element_type=_F32)


def _silu(x):
    return x * jax.nn.sigmoid(x)


def _rms(x, row):
    return x * lax.rsqrt(jnp.mean(x * x, axis=-1, keepdims=True) + EPS) * row


def _pre(x, mod_ref, g):
    return _rms(x, g * (1.0 + mod_ref[0, 1:2, :])) + mod_ref[0, 0:1, :]


def _post(y, mod_ref, g, weight=1.0):
    return _rms(y, (weight * mod_ref[0, 2:3, :]) * g)


def _params(n_axes, vmem_bytes):
    limit = min(int(vmem_bytes), V7X_VMEM_BYTES - 4 * 1024 * 1024)
    return pltpu.CompilerParams(dimension_semantics=("arbitrary",) * n_axes, vmem_limit_bytes=limit)


def _nbytes(shape, dtype):
    return int(np.prod(shape)) * jnp.dtype(dtype).itemsize


def _mod_kernel(cond_ref, w_ref, b_ref, o_ref):
    a = _silu(cond_ref[...]).astype(_BF16)
    o_ref[0] = _dot(a, w_ref[0].astype(_BF16)) + b_ref[0]


def _ada_mod(cond, w_mod, b_mod):
    depth, d, n = w_mod.shape
    vmem = 2 * _nbytes((d, MOD_TN), _F32) + _nbytes((d, MOD_TN), _BF16) + 8 * 1024 * 1024
    return pl.pallas_call(
        _mod_kernel,
        grid=(depth, n // MOD_TN),
        in_specs=[
            pl.BlockSpec((MOD_ROWS, d), lambda i, j: (0, 0)),
            pl.BlockSpec((1, d, MOD_TN), lambda i, j: (i, 0, j)),
            pl.BlockSpec((1, 1, MOD_TN), lambda i, j: (i, 0, j)),
        ],
        out_specs=pl.BlockSpec((1, MOD_ROWS, MOD_TN), lambda i, j: (i, 0, j)),
        out_shape=jax.ShapeDtypeStruct((depth, MOD_ROWS, n), _F32),
        compiler_params=_params(2, vmem),
        name="ada_mod",
    )(cond, w_mod, b_mod.reshape(depth, 1, n))


def _ffn_kernel(*refs, mixer_out):
    if mixer_out:
        (x_ref, mod_ref, g_ref, wgu_ref, wd_ref, ya_ref, yb_ref, mmod_ref, mg_ref, wo_ref,
         o_ref, h_ref, acc_ref, xin_ref) = refs
    else:
        x_ref, mod_ref, g_ref, wgu_ref, wd_ref, o_ref, h_ref, acc_ref = refs
        xin_ref = x_ref.at[0]
    tm = x_ref.shape[1]
    sub = min(FFN_SUB, tm)
    subs = [slice(s * sub, (s + 1) * sub) for s in range(tm // sub)]
    if mixer_out:
        for rows in subs:
            y = (_dot(ya_ref[0, rows, :], wo_ref[0, :CONV_CH, :])
                 + _dot(yb_ref[0, rows, :], wo_ref[0, CONV_CH:, :]))
            xin_ref[rows, :] = x_ref[0, rows, :] + _post(y, mmod_ref, mg_ref[...])
    for rows in subs:
        h_ref[rows, :] = _pre(xin_ref[rows, :], mod_ref, g_ref[0:1, :]).astype(_BF16)
    for rows in subs:
        for c in range(FFN_NCH):
            h = h_ref[rows, :]
            gate = _dot(h, wgu_ref[0, :, c * FFN_FC:(c + 1) * FFN_FC])
            up = _dot(h, wgu_ref[0, :, D_FF + c * FFN_FC:D_FF + (c + 1) * FFN_FC])
            act = (_silu(gate) * up).astype(_BF16)
            part = _dot(act, wd_ref[0, c * FFN_FC:(c + 1) * FFN_FC, :])
            if c == 0:
                acc_ref[rows, :] = part
            else:
                acc_ref[rows, :] += part
    for rows in subs:
        o_ref[0, rows, :] = xin_ref[rows, :] + _post(acc_ref[rows, :], mod_ref, g_ref[1:2, :], FFN_RES)


def _ffn(x, mod, g, wgu, wd, layer, shared_row=None, mixer=None):
    nb, t, d = x.shape
    tm = min(FFN_TM, t)
    mod_map = (lambda b, i: (b, 0, 0)) if shared_row is None else (lambda b, i: (shared_row, 0, 0))
    resident = pl.Buffered(1)
    wspec = lambda w: pl.BlockSpec((1,) + w.shape[1:], lambda b, i: (layer, 0, 0), pipeline_mode=resident)
    tok = lambda width: pl.BlockSpec((1, tm, width), lambda b, i: (b, i, 0))
    in_specs = [tok(d), pl.BlockSpec((1, 3, d), mod_map), pl.BlockSpec((2, d), lambda b, i: (0, 0)),
                wspec(wgu), wspec(wd)]
    args = [x, mod, g, wgu, wd]
    scratch = [pltpu.VMEM((tm, d), _BF16), pltpu.VMEM((tm, d), _F32)]
    vmem = (_nbytes(wgu.shape[1:], _BF16) + _nbytes(wd.shape[1:], _BF16) + 4 * _nbytes((tm, d), _F32)
            + _nbytes((tm, d), _BF16) + _nbytes((tm, d), _F32) + 8 * 1024 * 1024)
    if mixer is not None:
        ya, yb, mmod, mg, wo = mixer
        in_specs += [tok(ya.shape[-1]), tok(yb.shape[-1]), pl.BlockSpec((1, 3, d), lambda b, i: (b, 0, 0)),
                     pl.BlockSpec((1, d), lambda b, i: (0, 0)),
                     pl.BlockSpec((1,) + wo.shape[1:], lambda b, i: (0, 0, 0), pipeline_mode=resident)]
        args += [ya, yb, mmod, mg, wo]
        scratch.append(pltpu.VMEM((tm, d), _F32))
        vmem += (_nbytes(wo.shape[1:], _BF16) + _nbytes((tm, d), _F32)
                 + 2 * _nbytes((tm, ya.shape[-1] + yb.shape[-1]), _BF16))
    return pl.pallas_call(
        functools.partial(_ffn_kernel, mixer_out=mixer is not None),
        grid=(nb, t // tm),
        in_specs=in_specs,
        out_specs=tok(d),
        out_shape=jax.ShapeDtypeStruct(x.shape, _F32),
        scratch_shapes=scratch,
        compiler_params=_params(2, vmem),
        name="ffn",
    )(*args)


def _cast_kernel(w_ref, o_ref):
    o_ref[...] = w_ref[...].astype(_BF16)


def _cast_rows(r, c):
    limit = CAST_BLOCK_BYTES // (c * jnp.dtype(_F32).itemsize)
    return max(tr for tr in range(BF16_SUBLANES, r + 1, BF16_SUBLANES) if r % tr == 0 and tr <= limit)


def _to_bf16(w):
    layers, r, c = w.shape
    tr = _cast_rows(r, c)
    spec = pl.BlockSpec((1, tr, c), lambda l, i: (l, i, 0))
    return pl.pallas_call(
        _cast_kernel,
        grid=(layers, r // tr),
        in_specs=[spec],
        out_specs=spec,
        out_shape=jax.ShapeDtypeStruct(w.shape, _BF16),
        compiler_params=_params(2, 4 * CAST_BLOCK_BYTES + 8 * 1024 * 1024),
        name="to_bf16",
    )(w)


def _store_values_with_ones(v, v_ref):
    lane = lax.broadcasted_iota(jnp.int32, (v.shape[0], NA_VTILE), 1)
    first_head = lane < NA_HEAD_DIM
    for j in range(NA_HEADS // 2):
        pair = v[:, j * NA_VTILE:(j + 1) * NA_VTILE]
        v_ref[0, :, 2 * j * NA_VTILE:(2 * j + 1) * NA_VTILE] = jnp.where(first_head, pair, 1.0).astype(_BF16)
        v_ref[0, :, (2 * j + 1) * NA_VTILE:(2 * j + 2) * NA_VTILE] = jnp.where(first_head, 1.0, pair).astype(_BF16)


def _even_in_kernel(x_ref, mod_ref, g_ref, win_ref, wkt_ref, glu_ref, q_ref, kt_ref, v_ref):
    col = lambda j: slice(j * CONV_CH, (j + 1) * CONV_CH)
    h = _pre(x_ref[0], mod_ref, g_ref[...]).astype(_BF16)
    glu_ref[0] = _dot(h, win_ref[0, :, col(0)]) * jax.nn.sigmoid(_dot(h, win_ref[0, :, col(1)]))
    q_ref[0] = (_dot(h, win_ref[0, :, col(2)]) * (NA_HEAD_DIM ** -0.5)).astype(_BF16)
    _store_values_with_ones(_dot(h, win_ref[0, :, col(4)]), v_ref)
    kt = _dot_nt(wkt_ref[...], h)
    for j in range(kt.shape[1] // NA_TQ):
        kt_ref[0, j] = kt[:, j * NA_TQ:(j + 1) * NA_TQ].astype(_BF16)


def _even_in(x, mod, g, w_in, wkt):
    nb, t, d = x.shape
    tm = PROJ_TM
    vmem = 2 * _nbytes(w_in.shape, _BF16) + 8 * _nbytes((tm, d), _F32) + 16 * 1024 * 1024
    return pl.pallas_call(
        _even_in_kernel,
        grid=(nb, t // tm),
        in_specs=[
            pl.BlockSpec((1, tm, d), lambda b, i: (b, i, 0)),
            pl.BlockSpec((1, 3, d), lambda b, i: (b, 0, 0)),
            pl.BlockSpec((1, d), lambda b, i: (0, 0)),
            pl.BlockSpec(w_in.shape, lambda b, i: (0, 0, 0)),
            pl.BlockSpec(wkt.shape, lambda b, i: (0, 0)),
        ],
        out_specs=[
            pl.BlockSpec((1, tm, CONV_CH), lambda b, i: (b, i, 0)),
            pl.BlockSpec((1, tm, NA_WIDTH), lambda b, i: (b, i, 0)),
            pl.BlockSpec((1, tm // NA_TQ, NA_WIDTH, NA_TQ), lambda b, i: (b, i, 0, 0)),
            pl.BlockSpec((1, tm, NA_HEADS * NA_VTILE), lambda b, i: (b, i, 0)),
        ],
        out_shape=[
            jax.ShapeDtypeStruct((nb, t, CONV_CH), _F32),
            jax.ShapeDtypeStruct((nb, t, NA_WIDTH), _BF16),
            jax.ShapeDtypeStruct((nb, t // NA_TQ, NA_WIDTH, NA_TQ), _BF16),
            jax.ShapeDtypeStruct((nb, t, NA_HEADS * NA_VTILE), _BF16),
        ],
        compiler_params=_params(2, vmem),
        name="even_in",
    )(x, mod, g, w_in, wkt)


def _ctx_kv_kernel(x_ref, mod_ref, g_ref, win_ref, wkt_ref, kt_ref, v_ref):
    h = _pre(x_ref[0], mod_ref, g_ref[...]).astype(_BF16)
    kt_ref[0] = _dot_nt(wkt_ref[...], h).astype(_BF16)
    _store_values_with_ones(_dot(h, win_ref[0, :, 4 * CONV_CH:]), v_ref)


def _ctx_kv(x, mod, g, w_in, wkt, shared_row):
    nb, t, d = x.shape
    return pl.pallas_call(
        _ctx_kv_kernel,
        grid=(nb,),
        in_specs=[
            pl.BlockSpec((1, t, d), lambda b: (b, 0, 0)),
            pl.BlockSpec((1, 3, d), lambda b: (shared_row, 0, 0)),
            pl.BlockSpec((1, d), lambda b: (0, 0)),
            pl.BlockSpec(w_in.shape, lambda b: (0, 0, 0)),
            pl.BlockSpec(wkt.shape, lambda b: (0, 0)),
        ],
        out_specs=[
            pl.BlockSpec((1, NA_WIDTH, t), lambda b: (b, 0, 0)),
            pl.BlockSpec((1, t, NA_HEADS * NA_VTILE), lambda b: (b, 0, 0)),
        ],
        out_shape=[
            jax.ShapeDtypeStruct((nb, NA_WIDTH, t), _BF16),
            jax.ShapeDtypeStruct((nb, t, NA_HEADS * NA_VTILE), _BF16),
        ],
        compiler_params=_params(1, 24 * 1024 * 1024),
        name="ctx_kv",
    )(x, mod, g, w_in, wkt)


def _conv_kernel(main_ref, prev_ref, next_ref, w_ref, b_ref, lng_ref, lnb_ref, grp_ref, o_ref,
                 buf_ref, cv_ref):
    t = pl.program_id(1)
    nt = pl.num_programs(1)
    tc = main_ref.shape[1]
    buf_ref[0:CONV_HALO, :] = jnp.where(t > 0, prev_ref[0], 0.0)
    buf_ref[CONV_HALO:CONV_HALO + tc, :] = main_ref[0]
    buf_ref[CONV_HALO + tc:, :] = jnp.where(t < nt - 1, next_ref[0], 0.0)
    base = CONV_HALO - CONV_WIDTH // 2

    def chunk(r, carry):
        r0 = pl.multiple_of(r * CONV_RC, CONV_RC)
        acc = None
        for s in range(V7X_SUBLANES):
            part = None
            for a in range((base + CONV_WIDTH - 1) // V7X_SUBLANES + 1):
                k = V7X_SUBLANES * a + s - base
                if 0 <= k < CONV_WIDTH:
                    win = buf_ref[pl.ds(r0 + V7X_SUBLANES * a, CONV_RC + V7X_SUBLANES), :]
                    term = w_ref[k:k + 1, :] * win
                    part = term if part is None else part + term
            shifted = part[s:s + CONV_RC, :]
            acc = shifted if acc is None else acc + shifted
        cv_ref[pl.ds(r0, CONV_RC), :] = acc + b_ref[...]
        return carry

    lax.fori_loop(0, tc // CONV_RC, chunk, 0)

    grp = grp_ref[...]
    inv_n = 1.0 / (CONV_CH // CONV_GROUPS)

    def group_mean(z):
        hi = z.astype(_BF16)
        lo = (z - hi.astype(_F32)).astype(_BF16)
        return (_dot(hi, grp) + _dot(lo, grp)) * inv_n

    v = cv_ref[...]
    dlt = v - group_mean(v)
    var = group_mean(dlt * dlt)
    vn = dlt * lax.rsqrt(var + EPS) * lng_ref[...] + lnb_ref[...]
    o_ref[0] = _silu(vn).astype(_BF16)


def _conformer_conv(glu, dw_w, dw_b, ln_g, ln_b):
    nb, t, ch = glu.shape
    tc = CONV_TC
    hb = tc // CONV_HALO
    nh = t // CONV_HALO
    gid = np.arange(ch) // (ch // CONV_GROUPS)
    grp = jnp.asarray(gid[:, None] == gid[None, :], _BF16)
    row = lambda a: a.reshape(1, ch)
    rspec = pl.BlockSpec((1, ch), lambda b, i: (0, 0))
    return pl.pallas_call(
        _conv_kernel,
        grid=(nb, t // tc),
        in_specs=[
            pl.BlockSpec((1, tc, ch), lambda b, i: (b, i, 0)),
            pl.BlockSpec((1, CONV_HALO, ch), lambda b, i: (b, jnp.maximum(i * hb - 1, 0), 0)),
            pl.BlockSpec((1, CONV_HALO, ch), lambda b, i: (b, jnp.minimum((i + 1) * hb, nh - 1), 0)),
            pl.BlockSpec((CONV_WIDTH, ch), lambda b, i: (0, 0)),
            rspec, rspec, rspec,
            pl.BlockSpec((ch, ch), lambda b, i: (0, 0)),
        ],
        out_specs=pl.BlockSpec((1, tc, ch), lambda b, i: (b, i, 0)),
        out_shape=jax.ShapeDtypeStruct((nb, t, ch), _BF16),
        scratch_shapes=[pltpu.VMEM((tc + 2 * CONV_HALO, ch), _F32), pltpu.VMEM((tc, ch), _F32)],
        compiler_params=_params(2, 32 * 1024 * 1024),
        name="conformer_conv",
    )(glu, glu, glu, dw_w, row(dw_b), row(ln_g), row(ln_b), grp)


def _na_bias_table(rpb):
    rows = GRID_W
    pairs = rows // NA_QROWS
    reps = [0, 1, 2, pairs - 2, pairs - 1]
    heads, n_rr, n_cc = rpb.shape
    off = np.arange(GRID_W)[None, :] - np.arange(GRID_W)[:, None] + NA_WIN_C - 1
    onehot = (off[None] == np.arange(n_cc)[:, None, None]).reshape(n_cc, GRID_W * GRID_W)
    toe = jnp.einsum("hrk,kn->hrn", rpb.astype(_F32), jnp.asarray(onehot, _F32),
                     precision=lax.Precision.HIGHEST).reshape(heads, n_rr, GRID_W, GRID_W)
    qr = np.arange(NA_QROWS)[:, None, None, None]
    w = np.arange(GRID_W)[None, :, None, None]
    i = np.arange(NA_KROWS)[None, None, :, None]
    c = np.arange(GRID_W)[None, None, None, :]
    rr_sel, ok = [], []
    for p in reps:
        kb = np.clip(NA_QROWS * p - NA_WIN_R // 2, 0, rows - NA_KROWS)
        r = NA_QROWS * p + qr
        rs = np.clip(r - NA_WIN_R // 2, 0, rows - NA_WIN_R)
        cs = np.clip(w - NA_WIN_C // 2, 0, GRID_W - NA_WIN_C)
        krow = kb + i
        valid = (krow >= rs) & (krow < rs + NA_WIN_R) & (c >= cs) & (c < cs + NA_WIN_C)
        ok.append(np.broadcast_to(valid, (NA_QROWS, GRID_W, NA_KROWS, GRID_W)))
        rr_sel.append(np.clip(krow - r + NA_WIN_R - 1, 0, n_rr - 1)[:, 0, :, 0])
    ok = np.stack(ok).reshape(len(reps), 1, NA_TQ, NA_TK)
    rr_flat = np.stack(rr_sel).reshape(-1)
    sel = jnp.stack([toe[:, int(rr)] for rr in rr_flat], axis=1)
    sel = sel.reshape(heads, len(reps), NA_QROWS, NA_KROWS, GRID_W, GRID_W)
    sel = sel.transpose(1, 0, 2, 4, 3, 5).reshape(len(reps), heads, NA_TQ, NA_TK)
    return jnp.where(ok, sel, NA_NEG)


def _na_kernel(q_ref, kt_ref, v_ref, kct_ref, vc_ref, *rest):
    bias_refs, o_ref = rest[:-1], rest[-1]
    step = pl.program_id(1)
    n_blocks = kt_ref.shape[1]
    lane = lax.broadcasted_iota(jnp.int32, (NA_TQ, 2 * NA_HEAD_DIM), 1)
    first_head = lane < NA_HEAD_DIM
    n_pairs = len(bias_refs)

    def scores(u):
        pp, h = divmod(u, NA_HEADS)
        jb = jnp.clip(step * n_pairs + pp - (NA_WIN_R // 2) // NA_QROWS, 0, n_blocks - NA_KROWS // NA_QROWS)
        ls = slice((h // 2) * 2 * NA_HEAD_DIM, (h // 2 + 1) * 2 * NA_HEAD_DIM)
        q2 = q_ref[0, pp * NA_TQ:(pp + 1) * NA_TQ, ls]
        qh = jnp.where(first_head if h % 2 == 0 else jnp.logical_not(first_head), q2, jnp.zeros_like(q2))
        kt_win = jnp.concatenate([kt_ref[0, jb + j, ls, :] for j in range(NA_KROWS // NA_QROWS)], axis=1)
        return _dot(qh, kt_win) + bias_refs[pp][0, h], _dot(qh, kct_ref[0, ls, :])

    pending = scores(0)
    tiles = []
    for u in range(n_pairs * NA_HEADS):
        pp, h = divmod(u, NA_HEADS)
        s_w, s_c = pending
        if u + 1 < n_pairs * NA_HEADS:
            pending = scores(u + 1)
        m = jnp.maximum(jnp.max(s_w, axis=-1, keepdims=True), jnp.max(s_c, axis=-1, keepdims=True))
        e_w = jnp.exp(s_w - m).astype(_BF16)
        e_c = jnp.exp(s_c - m).astype(_BF16)
        jb = jnp.clip(step * n_pairs + pp - (NA_WIN_R // 2) // NA_QROWS, 0, n_blocks - NA_KROWS // NA_QROWS)
        win_rows = pl.ds(pl.multiple_of(jb * NA_TQ, NA_TQ), NA_TK)
        hl = slice(h * NA_VTILE, (h + 1) * NA_VTILE)
        tiles.append(_dot(e_w, v_ref[0, win_rows, hl]) + _dot(e_c, vc_ref[0, :, hl]))
        if h % 2 == 1:
            num = jnp.where(first_head, tiles[u - 1], tiles[u])
            den = jnp.where(first_head, pltpu.roll(tiles[u - 1], NA_HEAD_DIM, 1),
                            pltpu.roll(tiles[u], NA_HEAD_DIM, 1))
            ls = slice((h // 2) * 2 * NA_HEAD_DIM, (h // 2 + 1) * 2 * NA_HEAD_DIM)
            o_ref[0, pp * NA_TQ:(pp + 1) * NA_TQ, ls] = (num / den).astype(_BF16)


def _neighbourhood_attention(q, kt, v, kct, vc, bias):
    nb, t, width = q.shape
    n_blocks = kt.shape[1]
    lc = vc.shape[1]
    pairs = t // NA_TQ
    n_types = bias.shape[0]

    def bias_spec(pp):
        def index(b, s):
            p = s * NA_STEP_PAIRS + pp
            edge = (n_types - 1) // 2
            kind = jnp.where(p < edge, p, jnp.where(p >= pairs - edge, p - (pairs - n_types), edge))
            return (kind, 0, 0, 0)
        return pl.BlockSpec((1,) + bias.shape[1:], index)

    tq = NA_STEP_PAIRS * NA_TQ
    vmem = (2 * (_nbytes((t, width), _BF16) + _nbytes(v.shape[1:], _BF16)
                 + NA_STEP_PAIRS * _nbytes(bias.shape[1:], _F32)) + 16 * 1024 * 1024)
    return pl.pallas_call(
        _na_kernel,
        grid=(nb, pairs // NA_STEP_PAIRS),
        in_specs=[
            pl.BlockSpec((1, tq, width), lambda b, s: (b, s, 0)),
            pl.BlockSpec((1, n_blocks, width, NA_TQ), lambda b, s: (b, 0, 0, 0)),
            pl.BlockSpec((1,) + v.shape[1:], lambda b, s: (b, 0, 0)),
            pl.BlockSpec((1, width, lc), lambda b, s: (b, 0, 0)),
            pl.BlockSpec((1,) + vc.shape[1:], lambda b, s: (b, 0, 0)),
        ] + [bias_spec(pp) for pp in range(NA_STEP_PAIRS)],
        out_specs=pl.BlockSpec((1, tq, width), lambda b, s: (b, s, 0)),
        out_shape=jax.ShapeDtypeStruct((nb, t, width), _BF16),
        compiler_params=_params(2, vmem),
        name="na_attention",
    )(q, kt, v, kct, vc, *([bias] * NA_STEP_PAIRS))


def _odd_kernel(x_ref, xp_ref, xn_ref, mod_ref, g_ref, win_ref, cw_ref, wo_ref, o_ref):
    t = pl.program_id(1)
    nt = pl.num_programs(1)
    x = x_ref[0]
    tm = x.shape[0]
    n = tm + 2 * ODD_HALO
    xa = jnp.concatenate([xp_ref[0], x, xn_ref[0]], axis=0)
    ha = _pre(xa, mod_ref, g_ref[0:1, :]).astype(_BF16)
    d = x.shape[1]
    cz = _dot(ha, win_ref[0, :, d:2 * d]) * _dot(ha, win_ref[0, :, 2 * d:])
    row = lax.broadcasted_iota(jnp.int32, (n, 1), 0)
    inside = jnp.logical_and(jnp.logical_or(row >= ODD_HALO, t > 0),
                             jnp.logical_or(row < tm + ODD_HALO, t < nt - 1))
    cz = jnp.where(inside, cz, 0.0)
    mid = slice(ODD_HALO, ODD_HALO + tm)
    y = (cw_ref[0:1, :] * pltpu.roll(cz, 1, 0)[mid]
         + cw_ref[1:2, :] * cz[mid]
         + cw_ref[2:3, :] * pltpu.roll(cz, n - 1, 0)[mid])
    z = (_dot(ha[mid], win_ref[0, :, :d]) * y).astype(_BF16)
    out = _dot(z, wo_ref[0])
    o_ref[0] = x + _post(out, mod_ref, g_ref[1:2, :])


def _odd_mixer(x, mod, g, w_in, cw, wo):
    nb, t, d = x.shape
    tm = PROJ_TM
    hb = tm // ODD_HALO
    nh = t // ODD_HALO
    wspec = lambda w: pl.BlockSpec(w.shape, lambda b, i: (0,) * w.ndim)
    vmem = (2 * (_nbytes(w_in.shape, _BF16) + _nbytes(wo.shape, _BF16))
            + 12 * _nbytes((tm + 2 * ODD_HALO, d), _F32) + 8 * 1024 * 1024)
    return pl.pallas_call(
        _odd_kernel,
        grid=(nb, t // tm),
        in_specs=[
            pl.BlockSpec((1, tm, d), lambda b, i: (b, i, 0)),
            pl.BlockSpec((1, ODD_HALO, d), lambda b, i: (b, jnp.maximum(i * hb - 1, 0), 0)),
            pl.BlockSpec((1, ODD_HALO, d), lambda b, i: (b, jnp.minimum((i + 1) * hb, nh - 1), 0)),
            pl.BlockSpec((1, 3, d), lambda b, i: (b, 0, 0)),
            pl.BlockSpec((2, d), lambda b, i: (0, 0)),
            wspec(w_in), wspec(cw), wspec(wo),
        ],
        out_specs=pl.BlockSpec((1, tm, d), lambda b, i: (b, i, 0)),
        out_shape=jax.ShapeDtypeStruct(x.shape, _F32),
        compiler_params=_params(2, vmem),
        name="odd_mixer",
    )(x, x, x, mod, g, w_in, cw, wo)


def kernel(x, c, ctx, c_ctx, w_mod, b_mod, norm_g, ff1_w_gu, ff1_w_down, ff2_w_gu, ff2_w_down,
           ev_w_in, ev_w_out, ev_dw_w, ev_dw_b, ev_ln_g, ev_ln_b, ev_rpb, od_w_in, od_conv_w, od_w_out):
    bsz, seq, d = x.shape
    lc = ctx.shape[1]
    ctx_row = bsz
    cond = jnp.concatenate([c, c_ctx[None, :], jnp.zeros((MOD_ROWS - bsz - 1, d), _F32)], axis=0)
    mods = _ada_mod(cond, w_mod, b_mod).reshape(w_mod.shape[0], MOD_ROWS, N_MOD, d)

    ff1_gu, ff1_dn = _to_bf16(ff1_w_gu), _to_bf16(ff1_w_down)
    ff2_gu, ff2_dn = _to_bf16(ff2_w_gu), _to_bf16(ff2_w_down)
    ev_in, ev_out = _to_bf16(ev_w_in), _to_bf16(ev_w_out)
    od_in, od_out = _to_bf16(od_w_in), _to_bf16(od_w_out)

    m, g = mods[0], norm_g[0]
    x = _ffn(x, m[:, 0:3], g[0:2], ff1_gu, ff1_dn, 0)
    x_ctx = _ffn(ctx.reshape(1, bsz * lc, d), m[:, 0:3], g[0:2], ff1_gu, ff1_dn, 0, shared_row=ctx_row)
    x_ctx = x_ctx.reshape(bsz, lc, d)

    wkt = ev_in[0, :, 3 * CONV_CH:4 * CONV_CH].T
    glu, q, kt, v = _even_in(x, m[:, 3:6], g[2:3], ev_in[:1], wkt)
    kct, vc = _ctx_kv(x_ctx, m[:, 3:6], g[2:3], ev_in[:1], wkt, shared_row=ctx_row)
    ya = _conformer_conv(glu, ev_dw_w[0], ev_dw_b[0], ev_ln_g[0], ev_ln_b[0])
    yb = _neighbourhood_attention(q, kt, v, kct, vc, _na_bias_table(ev_rpb[0]))
    x = _ffn(x, m[:, 6:9], g[4:6], ff2_gu, ff2_dn, 0, mixer=(ya, yb, m[:, 3:6], g[3:4], ev_out[:1]))

    m, g = mods[1], norm_g[1]
    x = _ffn(x, m[:, 0:3], g[0:2], ff1_gu, ff1_dn, 1)
    x = _odd_mixer(x, m[:, 3:6], g[2:4], od_in[:1], od_conv_w[0], od_out[:1])
    x = _ffn(x, m[:, 6:9], g[4:6], ff2_gu, ff2_dn, 1)
    return x
```

```python
import functools
import math

import jax
import jax.numpy as jnp
import numpy as np
from jax import lax
from jax.experimental import pallas as pl
from jax.experimental.pallas import tpu as pltpu

D_MODEL = 1024
D_FF = 2816
N_MOD = 9
EPS = 1e-6
FFN_RES = 0.5
GRID_W = 64
CONV_CH = 512
CONV_GROUPS = 8
CONV_WIDTH = 31
NA_HEADS = 8
NA_HEAD_DIM = 64
NA_WIDTH = NA_HEADS * NA_HEAD_DIM
NA_WIN_R = 8
NA_WIN_C = 16
SC_CONV = 3

V7X_LANES = 128
V7X_SUBLANES = 8
V7X_VMEM_BYTES = 64 * 1024 * 1024
BF16_SUBLANES = 16

MOD_ROWS = 16
MOD_TN = 1536
CAST_BLOCK_BYTES = 6 * 1024 * 1024
FFN_TM = 1024
FFN_SUB = 512
FFN_FC = 256
FFN_NCH = D_FF // FFN_FC
PROJ_TM = 512
EVEN_TM = 1024
CONV_TC = 512
CONV_HALO = 16
CONV_RC = 32
NA_QROWS = 2
NA_TQ = NA_QROWS * GRID_W
NA_KROWS = 10
NA_TK = NA_KROWS * GRID_W
NA_NEG = -1e30
LOG2E = math.log2(math.e)
NA_VTILE = 2 * NA_HEAD_DIM
NA_STEP_PAIRS = 4
ODD_HALO = 8

_BF16 = jnp.bfloat16
_F32 = jnp.float32


def _dot(a, b):
    return jnp.dot(a, b, preferred_element_type=_F32)


def _dot_nt(a, b):
    return lax.dot_general(a, b, (((1,), (1,)), ((), ())), preferred_element_type=_F32)


def _silu(x):
    return x * jax.nn.sigmoid(x)


def _rms(x, row):
    return x * lax.rsqrt(jnp.mean(x * x, axis=-1, keepdims=True) + EPS) * row


def _pre(x, mod_ref, g):
    return _rms(x, g * (1.0 + mod_ref[0, 1:2, :])) + mod_ref[0, 0:1, :]


def _post(y, mod_ref, g, weight=1.0):
    return _rms(y, (weight * mod_ref[0, 2:3, :]) * g)


def _params(n_axes, vmem_bytes):
    limit = min(int(vmem_bytes), V7X_VMEM_BYTES - 4 * 1024 * 1024)
    return pltpu.CompilerParams(dimension_semantics=("arbitrary",) * n_axes, vmem_limit_bytes=limit)


def _nbytes(shape, dtype):
    return int(np.prod(shape)) * jnp.dtype(dtype).itemsize


def _mod_kernel(cond_ref, w_ref, b_ref, o_ref):
    a = _silu(cond_ref[...]).astype(_BF16)
    o_ref[0] = _dot(a, w_ref[0].astype(_BF16)) + b_ref[0]


def _ada_mod(cond, w_mod, b_mod):
    depth, d, n = w_mod.shape
    vmem = 2 * _nbytes((d, MOD_TN), _F32) + _nbytes((d, MOD_TN), _BF16) + 8 * 1024 * 1024
    return pl.pallas_call(
        _mod_kernel,
        grid=(depth, n // MOD_TN),
        in_specs=[
            pl.BlockSpec((MOD_ROWS, d), lambda i, j: (0, 0)),
            pl.BlockSpec((1, d, MOD_TN), lambda i, j: (i, 0, j)),
            pl.BlockSpec((1, 1, MOD_TN), lambda i, j: (i, 0, j)),
        ],
        out_specs=pl.BlockSpec((1, MOD_ROWS, MOD_TN), lambda i, j: (i, 0, j)),
        out_shape=jax.ShapeDtypeStruct((depth, MOD_ROWS, n), _F32),
        compiler_params=_params(2, vmem),
        name="ada_mod",
    )(cond, w_mod, b_mod.reshape(depth, 1, n))


def _ffn_kernel(*refs, mixer_out):
    if mixer_out:
        (x_ref, mod_ref, g_ref, wgu_ref, wd_ref, ya_ref, yb_ref, mmod_ref, mg_ref, wo_ref,
         o_ref, h_ref, acc_ref, xin_ref) = refs
    else:
        x_ref, mod_ref, g_ref, wgu_ref, wd_ref, o_ref, h_ref, acc_ref = refs
        xin_ref = x_ref.at[0]
    tm = x_ref.shape[1]
    sub = min(FFN_SUB, tm)
    subs = [slice(s * sub, (s + 1) * sub) for s in range(tm // sub)]
    if mixer_out:
        for rows in subs:
            y = (_dot(ya_ref[0, rows, :], wo_ref[0, :CONV_CH, :])
                 + _dot(yb_ref[0, rows, :], wo_ref[0, CONV_CH:, :]))
            xin_ref[rows, :] = x_ref[0, rows, :] + _post(y, mmod_ref, mg_ref[...])
    for rows in subs:
        h_ref[rows, :] = _pre(xin_ref[rows, :], mod_ref, g_ref[0:1, :]).astype(_BF16)
    for rows in subs:
        for c in range(FFN_NCH):
            h = h_ref[rows, :]
            gate = _dot(h, wgu_ref[0, :, c * FFN_FC:(c + 1) * FFN_FC])
            up = _dot(h, wgu_ref[0, :, D_FF + c * FFN_FC:D_FF + (c + 1) * FFN_FC])
            act = (_silu(gate) * up).astype(_BF16)
            part = _dot(act, wd_ref[0, c * FFN_FC:(c + 1) * FFN_FC, :])
            if c == 0:
                acc_ref[rows, :] = part
            else:
                acc_ref[rows, :] += part
    for rows in subs:
        o_ref[0, rows, :] = xin_ref[rows, :] + _post(acc_ref[rows, :], mod_ref, g_ref[1:2, :], FFN_RES)


def _ffn(x, mod, g, wgu, wd, layer, shared_row=None, mixer=None):
    nb, t, d = x.shape
    tm = min(FFN_TM, t)
    mod_map = (lambda b, i: (b, 0, 0)) if shared_row is None else (lambda b, i: (shared_row, 0, 0))
    resident = pl.Buffered(1)
    wspec = lambda w: pl.BlockSpec((1,) + w.shape[1:], lambda b, i: (layer, 0, 0), pipeline_mode=resident)
    tok = lambda width: pl.BlockSpec((1, tm, width), lambda b, i: (b, i, 0))
    in_specs = [tok(d), pl.BlockSpec((1, 3, d), mod_map), pl.BlockSpec((2, d), lambda b, i: (0, 0)),
                wspec(wgu), wspec(wd)]
    args = [x, mod, g, wgu, wd]
    scratch = [pltpu.VMEM((tm, d), _BF16), pltpu.VMEM((tm, d), _F32)]
    vmem = (_nbytes(wgu.shape[1:], _BF16) + _nbytes(wd.shape[1:], _BF16) + 4 * _nbytes((tm, d), _F32)
            + _nbytes((tm, d), _BF16) + _nbytes((tm, d), _F32) + 8 * 1024 * 1024)
    if mixer is not None:
        ya, yb, mmod, mg, wo = mixer
        in_specs += [tok(ya.shape[-1]), tok(yb.shape[-1]), pl.BlockSpec((1, 3, d), lambda b, i: (b, 0, 0)),
                     pl.BlockSpec((1, d), lambda b, i: (0, 0)),
                     pl.BlockSpec((1,) + wo.shape[1:], lambda b, i: (0, 0, 0), pipeline_mode=resident)]
        args += [ya, yb, mmod, mg, wo]
        scratch.append(pltpu.VMEM((tm, d), _F32))
        vmem += (_nbytes(wo.shape[1:], _BF16) + _nbytes((tm, d), _F32)
                 + 2 * _nbytes((tm, ya.shape[-1] + yb.shape[-1]), _BF16))
    return pl.pallas_call(
        functools.partial(_ffn_kernel, mixer_out=mixer is not None),
        grid=(nb, t // tm),
        in_specs=in_specs,
        out_specs=tok(d),
        out_shape=jax.ShapeDtypeStruct(x.shape, _F32),
        scratch_shapes=scratch,
        compiler_params=_params(2, vmem),
        name="ffn",
    )(*args)


def _cast_kernel(w_ref, o_ref):
    o_ref[...] = w_ref[...].astype(_BF16)


def _cast_rows(r, c):
    limit = CAST_BLOCK_BYTES // (c * jnp.dtype(_F32).itemsize)
    return max(tr for tr in range(BF16_SUBLANES, r + 1, BF16_SUBLANES) if r % tr == 0 and tr <= limit)


def _to_bf16(w):
    layers, r, c = w.shape
    tr = _cast_rows(r, c)
    spec = pl.BlockSpec((1, tr, c), lambda l, i: (l, i, 0))
    return pl.pallas_call(
        _cast_kernel,
        grid=(layers, r // tr),
        in_specs=[spec],
        out_specs=spec,
        out_shape=jax.ShapeDtypeStruct(w.shape, _BF16),
        compiler_params=_params(2, 4 * CAST_BLOCK_BYTES + 8 * 1024 * 1024),
        name="to_bf16",
    )(w)


def _store_values_with_ones(v, v_ref):
    lane = lax.broadcasted_iota(jnp.int32, (v.shape[0], NA_VTILE), 1)
    first_head = lane < NA_HEAD_DIM
    for j in range(NA_HEADS // 2):
        pair = v[:, j * NA_VTILE:(j + 1) * NA_VTILE]
        v_ref[0, :, 2 * j * NA_VTILE:(2 * j + 1) * NA_VTILE] = jnp.where(first_head, pair, 1.0).astype(_BF16)
        v_ref[0, :, (2 * j + 1) * NA_VTILE:(2 * j + 2) * NA_VTILE] = jnp.where(first_head, 1.0, pair).astype(_BF16)


def _even_in_kernel(x_ref, mod_ref, g_ref, win_ref, wkt_ref, glu_ref, q_ref, kt_ref, v_ref):
    col = lambda j: slice(j * CONV_CH, (j + 1) * CONV_CH)
    h = _pre(x_ref[0], mod_ref, g_ref[...]).astype(_BF16)
    glu_ref[0] = _dot(h, win_ref[0, :, col(0)]) * jax.nn.sigmoid(_dot(h, win_ref[0, :, col(1)]))
    q_ref[0] = (_dot(h, win_ref[0, :, col(2)]) * (NA_HEAD_DIM ** -0.5 * LOG2E)).astype(_BF16)
    _store_values_with_ones(_dot(h, win_ref[0, :, col(4)]), v_ref)
    kt = _dot_nt(wkt_ref[...], h)
    for j in range(kt.shape[1] // NA_TQ):
        kt_ref[0, j] = kt[:, j * NA_TQ:(j + 1) * NA_TQ].astype(_BF16)


def _even_in(x, mod, g, w_in, wkt):
    nb, t, d = x.shape
    tm = EVEN_TM
    vmem = 2 * _nbytes(w_in.shape, _BF16) + 8 * _nbytes((tm, d), _F32) + 16 * 1024 * 1024
    return pl.pallas_call(
        _even_in_kernel,
        grid=(nb, t // tm),
        in_specs=[
            pl.BlockSpec((1, tm, d), lambda b, i: (b, i, 0)),
            pl.BlockSpec((1, 3, d), lambda b, i: (b, 0, 0)),
            pl.BlockSpec((1, d), lambda b, i: (0, 0)),
            pl.BlockSpec(w_in.shape, lambda b, i: (0, 0, 0)),
            pl.BlockSpec(wkt.shape, lambda b, i: (0, 0)),
        ],
        out_specs=[
            pl.BlockSpec((1, tm, CONV_CH), lambda b, i: (b, i, 0)),
            pl.BlockSpec((1, tm, NA_WIDTH), lambda b, i: (b, i, 0)),
            pl.BlockSpec((1, tm // NA_TQ, NA_WIDTH, NA_TQ), lambda b, i: (b, i, 0, 0)),
            pl.BlockSpec((1, tm, NA_HEADS * NA_VTILE), lambda b, i: (b, i, 0)),
        ],
        out_shape=[
            jax.ShapeDtypeStruct((nb, t, CONV_CH), _F32),
            jax.ShapeDtypeStruct((nb, t, NA_WIDTH), _BF16),
            jax.ShapeDtypeStruct((nb, t // NA_TQ, NA_WIDTH, NA_TQ), _BF16),
            jax.ShapeDtypeStruct((nb, t, NA_HEADS * NA_VTILE), _BF16),
        ],
        compiler_params=_params(2, vmem),
        name="even_in",
    )(x, mod, g, w_in, wkt)


def _ctx_kv_kernel(x_ref, mod_ref, g_ref, win_ref, wkt_ref, kt_ref, v_ref):
    h = _pre(x_ref[0], mod_ref, g_ref[...]).astype(_BF16)
    kt_ref[0] = _dot_nt(wkt_ref[...], h).astype(_BF16)
    _store_values_with_ones(_dot(h, win_ref[0, :, 4 * CONV_CH:]), v_ref)


def _ctx_kv(x, mod, g, w_in, wkt, shared_row):
    nb, t, d = x.shape
    return pl.pallas_call(
        _ctx_kv_kernel,
        grid=(nb,),
        in_specs=[
            pl.BlockSpec((1, t, d), lambda b: (b, 0, 0)),
            pl.BlockSpec((1, 3, d), lambda b: (shared_row, 0, 0)),
            pl.BlockSpec((1, d), lambda b: (0, 0)),
            pl.BlockSpec(w_in.shape, lambda b: (0, 0, 0)),
            pl.BlockSpec(wkt.shape, lambda b: (0, 0)),
        ],
        out_specs=[
            pl.BlockSpec((1, NA_WIDTH, t), lambda b: (b, 0, 0)),
            pl.BlockSpec((1, t, NA_HEADS * NA_VTILE), lambda b: (b, 0, 0)),
        ],
        out_shape=[
            jax.ShapeDtypeStruct((nb, NA_WIDTH, t), _BF16),
            jax.ShapeDtypeStruct((nb, t, NA_HEADS * NA_VTILE), _BF16),
        ],
        compiler_params=_params(1, 24 * 1024 * 1024),
        name="ctx_kv",
    )(x, mod, g, w_in, wkt)


def _conv_kernel(main_ref, prev_ref, next_ref, w_ref, b_ref, lng_ref, lnb_ref, grp_ref, o_ref,
                 buf_ref, cv_ref):
    t = pl.program_id(1)
    nt = pl.num_programs(1)
    tc = main_ref.shape[1]
    buf_ref[0:CONV_HALO, :] = jnp.where(t > 0, prev_ref[0], 0.0)
    buf_ref[CONV_HALO:CONV_HALO + tc, :] = main_ref[0]
    buf_ref[CONV_HALO + tc:, :] = jnp.where(t < nt - 1, next_ref[0], 0.0)
    base = CONV_HALO - CONV_WIDTH // 2

    def chunk(r, carry):
        r0 = pl.multiple_of(r * CONV_RC, CONV_RC)
        acc = None
        for s in range(V7X_SUBLANES):
            part = None
            for a in range((base + CONV_WIDTH - 1) // V7X_SUBLANES + 1):
                k = V7X_SUBLANES * a + s - base
                if 0 <= k < CONV_WIDTH:
                    win = buf_ref[pl.ds(r0 + V7X_SUBLANES * a, CONV_RC + V7X_SUBLANES), :]
                    term = w_ref[k:k + 1, :] * win
                    part = term if part is None else part + term
            shifted = part[s:s + CONV_RC, :]
            acc = shifted if acc is None else acc + shifted
        cv_ref[pl.ds(r0, CONV_RC), :] = acc + b_ref[...]
        return carry

    lax.fori_loop(0, tc // CONV_RC, chunk, 0)

    grp = grp_ref[...]
    inv_n = 1.0 / (CONV_CH // CONV_GROUPS)

    def group_mean(z):
        hi = z.astype(_BF16)
        lo = (z - hi.astype(_F32)).astype(_BF16)
        return (_dot(hi, grp) + _dot(lo, grp)) * inv_n

    v = cv_ref[...]
    dlt = v - group_mean(v)
    var = _dot((dlt * dlt).astype(_BF16), grp) * inv_n
    vn = dlt * lax.rsqrt(var + EPS) * lng_ref[...] + lnb_ref[...]
    o_ref[0] = _silu(vn).astype(_BF16)


def _conformer_conv(glu, dw_w, dw_b, ln_g, ln_b):
    nb, t, ch = glu.shape
    tc = CONV_TC
    hb = tc // CONV_HALO
    nh = t // CONV_HALO
    gid = np.arange(ch) // (ch // CONV_GROUPS)
    grp = jnp.asarray(gid[:, None] == gid[None, :], _BF16)
    row = lambda a: a.reshape(1, ch)
    rspec = pl.BlockSpec((1, ch), lambda b, i: (0, 0))
    return pl.pallas_call(
        _conv_kernel,
        grid=(nb, t // tc),
        in_specs=[
            pl.BlockSpec((1, tc, ch), lambda b, i: (b, i, 0)),
            pl.BlockSpec((1, CONV_HALO, ch), lambda b, i: (b, jnp.maximum(i * hb - 1, 0), 0)),
            pl.BlockSpec((1, CONV_HALO, ch), lambda b, i: (b, jnp.minimum((i + 1) * hb, nh - 1), 0)),
            pl.BlockSpec((CONV_WIDTH, ch), lambda b, i: (0, 0)),
            rspec, rspec, rspec,
            pl.BlockSpec((ch, ch), lambda b, i: (0, 0)),
        ],
        out_specs=pl.BlockSpec((1, tc, ch), lambda b, i: (b, i, 0)),
        out_shape=jax.ShapeDtypeStruct((nb, t, ch), _BF16),
        scratch_shapes=[pltpu.VMEM((tc + 2 * CONV_HALO, ch), _F32), pltpu.VMEM((tc, ch), _F32)],
        compiler_params=_params(2, 32 * 1024 * 1024),
        name="conformer_conv",
    )(glu, glu, glu, dw_w, row(dw_b), row(ln_g), row(ln_b), grp)


def _na_bias_table(rpb):
    rows = GRID_W
    pairs = rows // NA_QROWS
    reps = [0, 1, 2, pairs - 2, pairs - 1]
    heads, n_rr, n_cc = rpb.shape
    off = np.arange(GRID_W)[None, :] - np.arange(GRID_W)[:, None] + NA_WIN_C - 1
    onehot = (off[None] == np.arange(n_cc)[:, None, None]).reshape(n_cc, GRID_W * GRID_W)
    toe = jnp.einsum("hrk,kn->hrn", rpb.astype(_F32), jnp.asarray(onehot, _F32),
                     precision=lax.Precision.HIGHEST).reshape(heads, n_rr, GRID_W, GRID_W)
    qr = np.arange(NA_QROWS)[:, None, None, None]
    w = np.arange(GRID_W)[None, :, None, None]
    i = np.arange(NA_KROWS)[None, None, :, None]
    c = np.arange(GRID_W)[None, None, None, :]
    rr_sel, ok = [], []
    for p in reps:
        kb = np.clip(NA_QROWS * p - NA_WIN_R // 2, 0, rows - NA_KROWS)
        r = NA_QROWS * p + qr
        rs = np.clip(r - NA_WIN_R // 2, 0, rows - NA_WIN_R)
        cs = np.clip(w - NA_WIN_C // 2, 0, GRID_W - NA_WIN_C)
        krow = kb + i
        valid = (krow >= rs) & (krow < rs + NA_WIN_R) & (c >= cs) & (c < cs + NA_WIN_C)
        ok.append(np.broadcast_to(valid, (NA_QROWS, GRID_W, NA_KROWS, GRID_W)))
        rr_sel.append(np.clip(krow - r + NA_WIN_R - 1, 0, n_rr - 1)[:, 0, :, 0])
    ok = np.stack(ok).reshape(len(reps), 1, NA_TQ, NA_TK)
    rr_flat = np.stack(rr_sel).reshape(-1)
    sel = jnp.stack([toe[:, int(rr)] for rr in rr_flat], axis=1)
    sel = sel.reshape(heads, len(reps), NA_QROWS, NA_KROWS, GRID_W, GRID_W)
    sel = sel.transpose(1, 0, 2, 4, 3, 5).reshape(len(reps), heads, NA_TQ, NA_TK)
    return jnp.where(ok, sel * LOG2E, NA_NEG)


def _na_kernel(q_ref, kt_ref, v_ref, kct_ref, vc_ref, *rest):
    bias_refs, o_ref = rest[:-1], rest[-1]
    step = pl.program_id(1)
    n_blocks = kt_ref.shape[1]
    lane = lax.broadcasted_iota(jnp.int32, (NA_TQ, 2 * NA_HEAD_DIM), 1)
    first_head = lane < NA_HEAD_DIM
    n_pairs = len(bias_refs)

    def scores(u):
        pp, h = divmod(u, NA_HEADS)
        jb = jnp.clip(step * n_pairs + pp - (NA_WIN_R // 2) // NA_QROWS, 0, n_blocks - NA_KROWS // NA_QROWS)
        ls = slice((h // 2) * 2 * NA_HEAD_DIM, (h // 2 + 1) * 2 * NA_HEAD_DIM)
        q2 = q_ref[0, pp * NA_TQ:(pp + 1) * NA_TQ, ls]
        qh = jnp.where(first_head if h % 2 == 0 else jnp.logical_not(first_head), q2, jnp.zeros_like(q2))
        kt_win = jnp.concatenate([kt_ref[0, jb + j, ls, :] for j in range(NA_KROWS // NA_QROWS)], axis=1)
        return _dot(qh, kt_win) + bias_refs[pp][0, h], _dot(qh, kct_ref[0, ls, :])

    pending = scores(0)
    tiles = []
    for u in range(n_pairs * NA_HEADS):
        pp, h = divmod(u, NA_HEADS)
        s_w, s_c = pending
        if u + 1 < n_pairs * NA_HEADS:
            pending = scores(u + 1)
        m = jnp.maximum(jnp.max(s_w, axis=-1, keepdims=True), jnp.max(s_c, axis=-1, keepdims=True))
        e_w = jnp.exp2(s_w - m).astype(_BF16)
        e_c = jnp.exp2(s_c - m).astype(_BF16)
        jb = jnp.clip(step * n_pairs + pp - (NA_WIN_R // 2) // NA_QROWS, 0, n_blocks - NA_KROWS // NA_QROWS)
        win_rows = pl.ds(pl.multiple_of(jb * NA_TQ, NA_TQ), NA_TK)
        hl = slice(h * NA_VTILE, (h + 1) * NA_VTILE)
        tiles.append(_dot(e_w, v_ref[0, win_rows, hl]) + _dot(e_c, vc_ref[0, :, hl]))
        if h % 2 == 1:
            num = jnp.where(first_head, tiles[u - 1], tiles[u])
            den = jnp.where(first_head, pltpu.roll(tiles[u - 1], NA_HEAD_DIM, 1),
                            pltpu.roll(tiles[u], NA_HEAD_DIM, 1))
            ls = slice((h // 2) * 2 * NA_HEAD_DIM, (h // 2 + 1) * 2 * NA_HEAD_DIM)
            o_ref[0, pp * NA_TQ:(pp + 1) * NA_TQ, ls] = (num / den).astype(_BF16)


def _neighbourhood_attention(q, kt, v, kct, vc, bias):
    nb, t, width = q.shape
    n_blocks = kt.shape[1]
    lc = vc.shape[1]
    pairs = t // NA_TQ
    n_types = bias.shape[0]

    def bias_spec(pp):
        def index(b, s):
            p = s * NA_STEP_PAIRS + pp
            edge = (n_types - 1) // 2
            kind = jnp.where(p < edge, p, jnp.where(p >= pairs - edge, p - (pairs - n_types), edge))
            return (kind, 0, 0, 0)
        return pl.BlockSpec((1,) + bias.shape[1:], index)

    tq = NA_STEP_PAIRS * NA_TQ
    vmem = (2 * (_nbytes((t, width), _BF16) + _nbytes(v.shape[1:], _BF16)
                 + NA_STEP_PAIRS * _nbytes(bias.shape[1:], _F32)) + 16 * 1024 * 1024)
    return pl.pallas_call(
        _na_kernel,
        grid=(nb, pairs // NA_STEP_PAIRS),
        in_specs=[
            pl.BlockSpec((1, tq, width), lambda b, s: (b, s, 0)),
            pl.BlockSpec((1, n_blocks, width, NA_TQ), lambda b, s: (b, 0, 0, 0)),
            pl.BlockSpec((1,) + v.shape[1:], lambda b, s: (b, 0, 0)),
            pl.BlockSpec((1, width, lc), lambda b, s: (b, 0, 0)),
            pl.BlockSpec((1,) + vc.shape[1:], lambda b, s: (b, 0, 0)),
        ] + [bias_spec(pp) for pp in range(NA_STEP_PAIRS)],
        out_specs=pl.BlockSpec((1, tq, width), lambda b, s: (b, s, 0)),
        out_shape=jax.ShapeDtypeStruct((nb, t, width), _BF16),
        compiler_params=_params(2, vmem),
        name="na_attention",
    )(q, kt, v, kct, vc, *([bias] * NA_STEP_PAIRS))


def _odd_kernel(x_ref, xp_ref, xn_ref, mod_ref, g_ref, win_ref, cw_ref, wo_ref, o_ref):
    t = pl.program_id(1)
    nt = pl.num_programs(1)
    x = x_ref[0]
    tm = x.shape[0]
    n = tm + 2 * ODD_HALO
    xa = jnp.concatenate([xp_ref[0], x, xn_ref[0]], axis=0)
    ha = _pre(xa, mod_ref, g_ref[0:1, :]).astype(_BF16)
    d = x.shape[1]
    cz = _dot(ha, win_ref[0, :, d:2 * d]) * _dot(ha, win_ref[0, :, 2 * d:])
    row = lax.broadcasted_iota(jnp.int32, (n, 1), 0)
    inside = jnp.logical_and(jnp.logical_or(row >= ODD_HALO, t > 0),
                             jnp.logical_or(row < tm + ODD_HALO, t < nt - 1))
    cz = jnp.where(inside, cz, 0.0)
    mid = slice(ODD_HALO, ODD_HALO + tm)
    y = (cw_ref[0:1, :] * pltpu.roll(cz, 1, 0)[mid]
         + cw_ref[1:2, :] * cz[mid]
         + cw_ref[2:3, :] * pltpu.roll(cz, n - 1, 0)[mid])
    z = (_dot(ha[mid], win_ref[0, :, :d]) * y).astype(_BF16)
    out = _dot(z, wo_ref[0])
    o_ref[0] = x + _post(out, mod_ref, g_ref[1:2, :])


def _odd_mixer(x, mod, g, w_in, cw, wo):
    nb, t, d = x.shape
    tm = PROJ_TM
    hb = tm // ODD_HALO
    nh = t // ODD_HALO
    wspec = lambda w: pl.BlockSpec(w.shape, lambda b, i: (0,) * w.ndim)
    vmem = (2 * (_nbytes(w_in.shape, _BF16) + _nbytes(wo.shape, _BF16))
            + 12 * _nbytes((tm + 2 * ODD_HALO, d), _F32) + 8 * 1024 * 1024)
    return pl.pallas_call(
        _odd_kernel,
        grid=(nb, t // tm),
        in_specs=[
            pl.BlockSpec((1, tm, d), lambda b, i: (b, i, 0)),
            pl.BlockSpec((1, ODD_HALO, d), lambda b, i: (b, jnp.maximum(i * hb - 1, 0), 0)),
            pl.BlockSpec((1, ODD_HALO, d), lambda b, i: (b, jnp.minimum((i + 1) * hb, nh - 1), 0)),
            pl.BlockSpec((1, 3, d), lambda b, i: (b, 0, 0)),
            pl.BlockSpec((2, d), lambda b, i: (0, 0)),
            wspec(w_in), wspec(cw), wspec(wo),
        ],
        out_specs=pl.BlockSpec((1, tm, d), lambda b, i: (b, i, 0)),
        out_shape=jax.ShapeDtypeStruct(x.shape, _F32),
        compiler_params=_params(2, vmem),
        name="odd_mixer",
    )(x, x, x, mod, g, w_in, cw, wo)


def kernel(x, c, ctx, c_ctx, w_mod, b_mod, norm_g, ff1_w_gu, ff1_w_down, ff2_w_gu, ff2_w_down,
           ev_w_in, ev_w_out, ev_dw_w, ev_dw_b, ev_ln_g, ev_ln_b, ev_rpb, od_w_in, od_conv_w, od_w_out):
    bsz, seq, d = x.shape
    lc = ctx.shape[1]
    ctx_row = bsz
    cond = jnp.concatenate([c, c_ctx[None, :], jnp.zeros((MOD_ROWS - bsz - 1, d), _F32)], axis=0)
    mods = _ada_mod(cond, w_mod, b_mod).reshape(w_mod.shape[0], MOD_ROWS, N_MOD, d)

    ff1_gu, ff1_dn = _to_bf16(ff1_w_gu), _to_bf16(ff1_w_down)
    ff2_gu, ff2_dn = _to_bf16(ff2_w_gu), _to_bf16(ff2_w_down)
    ev_in, ev_out = _to_bf16(ev_w_in), _to_bf16(ev_w_out)
    od_in, od_out = _to_bf16(od_w_in), _to_bf16(od_w_out)

    m, g = mods[0], norm_g[0]
    x = _ffn(x, m[:, 0:3], g[0:2], ff1_gu, ff1_dn, 0)
    x_ctx = _ffn(ctx.reshape(1, bsz * lc, d), m[:, 0:3], g[0:2], ff1_gu, ff1_dn, 0, shared_row=ctx_row)
    x_ctx = x_ctx.reshape(bsz, lc, d)

    wkt = ev_in[0, :, 3 * CONV_CH:4 * CONV_CH].T
    glu, q, kt, v = _even_in(x, m[:, 3:6], g[2:3], ev_in[:1], wkt)
    kct, vc = _ctx_kv(x_ctx, m[:, 3:6], g[2:3], ev_in[:1], wkt, shared_row=ctx_row)
    ya = _conformer_conv(glu, ev_dw_w[0], ev_dw_b[0], ev_ln_g[0], ev_ln_b[0])
    yb = _neighbourhood_attention(q, kt, v, kct, vc, _na_bias_table(ev_rpb[0]))
    x = _ffn(x, m[:, 6:9], g[4:6], ff2_gu, ff2_dn, 0, mixer=(ya, yb, m[:, 3:6], g[3:4], ev_out[:1]))

    m, g = mods[1], norm_g[1]
    x = _ffn(x, m[:, 0:3], g[0:2], ff1_gu, ff1_dn, 1)
    x = _odd_mixer(x, m[:, 3:6], g[2:4], od_in[:1], od_conv_w[0], od_out[:1])
    x = _ffn(x, m[:, 6:9], g[4:6], ff2_gu, ff2_dn, 1)
    return x
```

```python
import functools
import math

import jax
import jax.numpy as jnp
import numpy as np
from jax import lax
from jax.experimental import pallas as pl
from jax.experimental.pallas import tpu as pltpu

D_MODEL = 1024
D_FF = 2816
N_MOD = 9
EPS = 1e-6
FFN_RES = 0.5
GRID_W = 64
CONV_CH = 512
CONV_GROUPS = 8
CONV_WIDTH = 31
NA_HEADS = 8
NA_HEAD_DIM = 64
NA_WIDTH = NA_HEADS * NA_HEAD_DIM
NA_WIN_R = 8
NA_WIN_C = 16
SC_CONV = 3

V7X_LANES = 128
V7X_SUBLANES = 8
V7X_VMEM_BYTES = 64 * 1024 * 1024
BF16_SUBLANES = 16

MOD_ROWS = 16
MOD_TN = 1536
CAST_BLOCK_BYTES = 6 * 1024 * 1024
FFN_TM = 1024
FFN_SUB = 512
FFN_FC = 256
FFN_NCH = D_FF // FFN_FC
PROJ_TM = 512
EVEN_TM = 1024
CONV_TC = 512
CONV_HALO = 16
CONV_RC = 64
CONV_LC = 256
CONV_NORM_ROWS = 128
NA_QROWS = 2
NA_TQ = NA_QROWS * GRID_W
NA_KROWS = 10
NA_TK = NA_KROWS * GRID_W
NA_NEG = -1e30
LOG2E = math.log2(math.e)
NA_VTILE = 2 * NA_HEAD_DIM
NA_STEP_PAIRS = 4
ODD_HALO = 8

_BF16 = jnp.bfloat16
_F32 = jnp.float32


def _dot(a, b):
    return jnp.dot(a, b, preferred_element_type=_F32)


def _dot_nt(a, b):
    return lax.dot_general(a, b, (((1,), (1,)), ((), ())), preferred_element_type=_F32)


def _silu(x):
    return x * jax.nn.sigmoid(x)


def _rms(x, row):
    return x * lax.rsqrt(jnp.mean(x * x, axis=-1, keepdims=True) + EPS) * row


def _pre(x, mod_ref, g):
    return _rms(x, g * (1.0 + mod_ref[0, 1:2, :])) + mod_ref[0, 0:1, :]


def _post(y, mod_ref, g, weight=1.0):
    return _rms(y, (weight * mod_ref[0, 2:3, :]) * g)


def _params(n_axes, vmem_bytes):
    limit = min(int(vmem_bytes), V7X_VMEM_BYTES - 4 * 1024 * 1024)
    return pltpu.CompilerParams(dimension_semantics=("arbitrary",) * n_axes, vmem_limit_bytes=limit)


def _nbytes(shape, dtype):
    return int(np.prod(shape)) * jnp.dtype(dtype).itemsize


def _mod_kernel(cond_ref, w_ref, b_ref, o_ref):
    a = _silu(cond_ref[...]).astype(_BF16)
    o_ref[0] = _dot(a, w_ref[0].astype(_BF16)) + b_ref[0]


def _ada_mod(cond, w_mod, b_mod):
    depth, d, n = w_mod.shape
    vmem = 2 * _nbytes((d, MOD_TN), _F32) + _nbytes((d, MOD_TN), _BF16) + 8 * 1024 * 1024
    return pl.pallas_call(
        _mod_kernel,
        grid=(depth, n // MOD_TN),
        in_specs=[
            pl.BlockSpec((MOD_ROWS, d), lambda i, j: (0, 0)),
            pl.BlockSpec((1, d, MOD_TN), lambda i, j: (i, 0, j)),
            pl.BlockSpec((1, 1, MOD_TN), lambda i, j: (i, 0, j)),
        ],
        out_specs=pl.BlockSpec((1, MOD_ROWS, MOD_TN), lambda i, j: (i, 0, j)),
        out_shape=jax.ShapeDtypeStruct((depth, MOD_ROWS, n), _F32),
        compiler_params=_params(2, vmem),
        name="ada_mod",
    )(cond, w_mod, b_mod.reshape(depth, 1, n))


def _ffn_kernel(*refs, mixer_out):
    if mixer_out:
        (x_ref, mod_ref, g_ref, wgu_ref, wd_ref, ya_ref, yb_ref, mmod_ref, mg_ref, wo_ref,
         o_ref, h_ref, acc_ref, xin_ref) = refs
    else:
        x_ref, mod_ref, g_ref, wgu_ref, wd_ref, o_ref, h_ref, acc_ref = refs
        xin_ref = x_ref.at[0]
    tm = x_ref.shape[1]
    sub = min(FFN_SUB, tm)
    subs = [slice(s * sub, (s + 1) * sub) for s in range(tm // sub)]
    if mixer_out:
        for rows in subs:
            y = (_dot(ya_ref[0, rows, :], wo_ref[0, :CONV_CH, :])
                 + _dot(yb_ref[0, rows, :], wo_ref[0, CONV_CH:, :]))
            xin_ref[rows, :] = x_ref[0, rows, :] + _post(y, mmod_ref, mg_ref[...])
    for rows in subs:
        h_ref[rows, :] = _pre(xin_ref[rows, :], mod_ref, g_ref[0:1, :]).astype(_BF16)
    for rows in subs:
        for c in range(FFN_NCH):
            h = h_ref[rows, :]
            gate = _dot(h, wgu_ref[0, :, c * FFN_FC:(c + 1) * FFN_FC])
            up = _dot(h, wgu_ref[0, :, D_FF + c * FFN_FC:D_FF + (c + 1) * FFN_FC])
            act = (_silu(gate) * up).astype(_BF16)
            part = _dot(act, wd_ref[0, c * FFN_FC:(c + 1) * FFN_FC, :])
            if c == 0:
                acc_ref[rows, :] = part
            else:
                acc_ref[rows, :] += part
    for rows in subs:
        o_ref[0, rows, :] = xin_ref[rows, :] + _post(acc_ref[rows, :], mod_ref, g_ref[1:2, :], FFN_RES)


def _ffn(x, mod, g, wgu, wd, layer, shared_row=None, mixer=None):
    nb, t, d = x.shape
    tm = min(FFN_TM, t)
    mod_map = (lambda b, i: (b, 0, 0)) if shared_row is None else (lambda b, i: (shared_row, 0, 0))
    resident = pl.Buffered(1)
    wspec = lambda w: pl.BlockSpec((1,) + w.shape[1:], lambda b, i: (layer, 0, 0), pipeline_mode=resident)
    tok = lambda width: pl.BlockSpec((1, tm, width), lambda b, i: (b, i, 0))
    in_specs = [tok(d), pl.BlockSpec((1, 3, d), mod_map), pl.BlockSpec((2, d), lambda b, i: (0, 0)),
                wspec(wgu), wspec(wd)]
    args = [x, mod, g, wgu, wd]
    scratch = [pltpu.VMEM((tm, d), _BF16), pltpu.VMEM((tm, d), _F32)]
    vmem = (_nbytes(wgu.shape[1:], _BF16) + _nbytes(wd.shape[1:], _BF16) + 4 * _nbytes((tm, d), _F32)
            + _nbytes((tm, d), _BF16) + _nbytes((tm, d), _F32) + 8 * 1024 * 1024)
    if mixer is not None:
        ya, yb, mmod, mg, wo = mixer
        in_specs += [tok(ya.shape[-1]), tok(yb.shape[-1]), pl.BlockSpec((1, 3, d), lambda b, i: (b, 0, 0)),
                     pl.BlockSpec((1, d), lambda b, i: (0, 0)),
                     pl.BlockSpec((1,) + wo.shape[1:], lambda b, i: (0, 0, 0), pipeline_mode=resident)]
        args += [ya, yb, mmod, mg, wo]
        scratch.append(pltpu.VMEM((tm, d), _F32))
        vmem += (_nbytes(wo.shape[1:], _BF16) + _nbytes((tm, d), _F32)
                 + 2 * _nbytes((tm, ya.shape[-1] + yb.shape[-1]), _BF16))
    return pl.pallas_call(
        functools.partial(_ffn_kernel, mixer_out=mixer is not None),
        grid=(nb, t // tm),
        in_specs=in_specs,
        out_specs=tok(d),
        out_shape=jax.ShapeDtypeStruct(x.shape, _F32),
        scratch_shapes=scratch,
        compiler_params=_params(2, vmem),
        name="ffn",
    )(*args)


def _cast_kernel(w_ref, o_ref):
    o_ref[...] = w_ref[...].astype(_BF16)


def _cast_rows(r, c):
    limit = CAST_BLOCK_BYTES // (c * jnp.dtype(_F32).itemsize)
    return max(tr for tr in range(BF16_SUBLANES, r + 1, BF16_SUBLANES) if r % tr == 0 and tr <= limit)


def _to_bf16(w):
    layers, r, c = w.shape
    tr = _cast_rows(r, c)
    spec = pl.BlockSpec((1, tr, c), lambda l, i: (l, i, 0))
    return pl.pallas_call(
        _cast_kernel,
        grid=(layers, r // tr),
        in_specs=[spec],
        out_specs=spec,
        out_shape=jax.ShapeDtypeStruct(w.shape, _BF16),
        compiler_params=_params(2, 4 * CAST_BLOCK_BYTES + 8 * 1024 * 1024),
        name="to_bf16",
    )(w)


def _store_values_with_ones(v, v_ref):
    lane = lax.broadcasted_iota(jnp.int32, (v.shape[0], NA_VTILE), 1)
    first_head = lane < NA_HEAD_DIM
    for j in range(NA_HEADS // 2):
        pair = v[:, j * NA_VTILE:(j + 1) * NA_VTILE]
        v_ref[0, :, 2 * j * NA_VTILE:(2 * j + 1) * NA_VTILE] = jnp.where(first_head, pair, 1.0).astype(_BF16)
        v_ref[0, :, (2 * j + 1) * NA_VTILE:(2 * j + 2) * NA_VTILE] = jnp.where(first_head, 1.0, pair).astype(_BF16)


def _even_in_kernel(x_ref, mod_ref, g_ref, win_ref, wkt_ref, glu_ref, q_ref, kt_ref, v_ref):
    col = lambda j: slice(j * CONV_CH, (j + 1) * CONV_CH)
    h = _pre(x_ref[0], mod_ref, g_ref[...]).astype(_BF16)
    glu_ref[0] = _dot(h, win_ref[0, :, col(0)]) * jax.nn.sigmoid(_dot(h, win_ref[0, :, col(1)]))
    q_ref[0] = (_dot(h, win_ref[0, :, col(2)]) * (NA_HEAD_DIM ** -0.5 * LOG2E)).astype(_BF16)
    _store_values_with_ones(_dot(h, win_ref[0, :, col(4)]), v_ref)
    kt = _dot_nt(wkt_ref[...], h)
    for j in range(kt.shape[1] // NA_TQ):
        kt_ref[0, j] = kt[:, j * NA_TQ:(j + 1) * NA_TQ].astype(_BF16)


def _even_in(x, mod, g, w_in, wkt):
    nb, t, d = x.shape
    tm = EVEN_TM
    vmem = 2 * _nbytes(w_in.shape, _BF16) + 8 * _nbytes((tm, d), _F32) + 16 * 1024 * 1024
    return pl.pallas_call(
        _even_in_kernel,
        grid=(nb, t // tm),
        in_specs=[
            pl.BlockSpec((1, tm, d), lambda b, i: (b, i, 0)),
            pl.BlockSpec((1, 3, d), lambda b, i: (b, 0, 0)),
            pl.BlockSpec((1, d), lambda b, i: (0, 0)),
            pl.BlockSpec(w_in.shape, lambda b, i: (0, 0, 0)),
            pl.BlockSpec(wkt.shape, lambda b, i: (0, 0)),
        ],
        out_specs=[
            pl.BlockSpec((1, tm, CONV_CH), lambda b, i: (b, i, 0)),
            pl.BlockSpec((1, tm, NA_WIDTH), lambda b, i: (b, i, 0)),
            pl.BlockSpec((1, tm // NA_TQ, NA_WIDTH, NA_TQ), lambda b, i: (b, i, 0, 0)),
            pl.BlockSpec((1, tm, NA_HEADS * NA_VTILE), lambda b, i: (b, i, 0)),
        ],
        out_shape=[
            jax.ShapeDtypeStruct((nb, t, CONV_CH), _F32),
            jax.ShapeDtypeStruct((nb, t, NA_WIDTH), _BF16),
            jax.ShapeDtypeStruct((nb, t // NA_TQ, NA_WIDTH, NA_TQ), _BF16),
            jax.ShapeDtypeStruct((nb, t, NA_HEADS * NA_VTILE), _BF16),
        ],
        compiler_params=_params(2, vmem),
        name="even_in",
    )(x, mod, g, w_in, wkt)


def _ctx_kv_kernel(x_ref, mod_ref, g_ref, win_ref, wkt_ref, kt_ref, v_ref):
    h = _pre(x_ref[0], mod_ref, g_ref[...]).astype(_BF16)
    kt_ref[0] = _dot_nt(wkt_ref[...], h).astype(_BF16)
    _store_values_with_ones(_dot(h, win_ref[0, :, 4 * CONV_CH:]), v_ref)


def _ctx_kv(x, mod, g, w_in, wkt, shared_row):
    nb, t, d = x.shape
    return pl.pallas_call(
        _ctx_kv_kernel,
        grid=(nb,),
        in_specs=[
            pl.BlockSpec((1, t, d), lambda b: (b, 0, 0)),
            pl.BlockSpec((1, 3, d), lambda b: (shared_row, 0, 0)),
            pl.BlockSpec((1, d), lambda b: (0, 0)),
            pl.BlockSpec(w_in.shape, lambda b: (0, 0, 0)),
            pl.BlockSpec(wkt.shape, lambda b: (0, 0)),
        ],
        out_specs=[
            pl.BlockSpec((1, NA_WIDTH, t), lambda b: (b, 0, 0)),
            pl.BlockSpec((1, t, NA_HEADS * NA_VTILE), lambda b: (b, 0, 0)),
        ],
        out_shape=[
            jax.ShapeDtypeStruct((nb, NA_WIDTH, t), _BF16),
            jax.ShapeDtypeStruct((nb, t, NA_HEADS * NA_VTILE), _BF16),
        ],
        compiler_params=_params(1, 24 * 1024 * 1024),
        name="ctx_kv",
    )(x, mod, g, w_in, wkt)


def _conv_kernel(main_ref, prev_ref, next_ref, w_ref, b_ref, lng_ref, lnb_ref, grp_ref, o_ref,
                 buf_ref, cv_ref):
    t = pl.program_id(1)
    nt = pl.num_programs(1)
    tc = main_ref.shape[1]
    buf_ref[0:CONV_HALO, :] = jnp.where(t > 0, prev_ref[0], 0.0)
    buf_ref[CONV_HALO:CONV_HALO + tc, :] = main_ref[0]
    buf_ref[CONV_HALO + tc:, :] = jnp.where(t < nt - 1, next_ref[0], 0.0)
    base = CONV_HALO - CONV_WIDTH // 2

    def conv_chunk(r0):
        for lc in range(CONV_CH // CONV_LC):
            lanes = slice(lc * CONV_LC, (lc + 1) * CONV_LC)
            acc = None
            for s in range(V7X_SUBLANES):
                part = None
                for a in range((base + CONV_WIDTH - 1) // V7X_SUBLANES + 1):
                    k = V7X_SUBLANES * a + s - base
                    if 0 <= k < CONV_WIDTH:
                        win = buf_ref[r0 + V7X_SUBLANES * a:r0 + V7X_SUBLANES * a + CONV_RC + V7X_SUBLANES, lanes]
                        term = w_ref[k:k + 1, lanes] * win
                        part = term if part is None else part + term
                shifted = part[s:s + CONV_RC, :]
                acc = shifted if acc is None else acc + shifted
            cv_ref[r0:r0 + CONV_RC, lanes] = acc + b_ref[:, lanes]

    grp = grp_ref[...]
    inv_n = 1.0 / (CONV_CH // CONV_GROUPS)

    def group_mean(z):
        hi = z.astype(_BF16)
        lo = (z - hi.astype(_F32)).astype(_BF16)
        return (_dot(hi, grp) + _dot(lo, grp)) * inv_n

    def group_norm(rows):
        v = cv_ref[rows, :]
        dlt = v - group_mean(v)
        var = _dot((dlt * dlt).astype(_BF16), grp) * inv_n
        vn = dlt * lax.rsqrt(var + EPS) * lng_ref[...] + lnb_ref[...]
        o_ref[0, rows, :] = _silu(vn).astype(_BF16)

    for blk in range(tc // CONV_NORM_ROWS):
        for r0 in range(blk * CONV_NORM_ROWS, (blk + 1) * CONV_NORM_ROWS, CONV_RC):
            conv_chunk(r0)
        group_norm(slice(blk * CONV_NORM_ROWS, (blk + 1) * CONV_NORM_ROWS))


def _conformer_conv(glu, dw_w, dw_b, ln_g, ln_b):
    nb, t, ch = glu.shape
    tc = CONV_TC
    hb = tc // CONV_HALO
    nh = t // CONV_HALO
    gid = np.arange(ch) // (ch // CONV_GROUPS)
    grp = jnp.asarray(gid[:, None] == gid[None, :], _BF16)
    row = lambda a: a.reshape(1, ch)
    rspec = pl.BlockSpec((1, ch), lambda b, i: (0, 0))
    return pl.pallas_call(
        _conv_kernel,
        grid=(nb, t // tc),
        in_specs=[
            pl.BlockSpec((1, tc, ch), lambda b, i: (b, i, 0)),
            pl.BlockSpec((1, CONV_HALO, ch), lambda b, i: (b, jnp.maximum(i * hb - 1, 0), 0)),
            pl.BlockSpec((1, CONV_HALO, ch), lambda b, i: (b, jnp.minimum((i + 1) * hb, nh - 1), 0)),
            pl.BlockSpec((CONV_WIDTH, ch), lambda b, i: (0, 0)),
            rspec, rspec, rspec,
            pl.BlockSpec((ch, ch), lambda b, i: (0, 0)),
        ],
        out_specs=pl.BlockSpec((1, tc, ch), lambda b, i: (b, i, 0)),
        out_shape=jax.ShapeDtypeStruct((nb, t, ch), _BF16),
        scratch_shapes=[pltpu.VMEM((tc + 2 * CONV_HALO, ch), _F32), pltpu.VMEM((tc, ch), _F32)],
        compiler_params=_params(2, 32 * 1024 * 1024),
        name="conformer_conv",
    )(glu, glu, glu, dw_w, row(dw_b), row(ln_g), row(ln_b), grp)


def _na_bias_table(rpb):
    rows = GRID_W
    pairs = rows // NA_QROWS
    reps = [0, 1, 2, pairs - 2, pairs - 1]
    heads, n_rr, n_cc = rpb.shape
    off = np.arange(GRID_W)[None, :] - np.arange(GRID_W)[:, None] + NA_WIN_C - 1
    onehot = (off[None] == np.arange(n_cc)[:, None, None]).reshape(n_cc, GRID_W * GRID_W)
    toe = jnp.einsum("hrk,kn->hrn", rpb.astype(_F32), jnp.asarray(onehot * LOG2E, _F32),
                     precision=lax.Precision.HIGHEST).reshape(heads, n_rr, GRID_W, GRID_W)
    qr = np.arange(NA_QROWS)[:, None, None, None]
    w = np.arange(GRID_W)[None, :, None, None]
    i = np.arange(NA_KROWS)[None, None, :, None]
    c = np.arange(GRID_W)[None, None, None, :]
    rr_sel, ok = [], []
    for p in reps:
        kb = np.clip(NA_QROWS * p - NA_WIN_R // 2, 0, rows - NA_KROWS)
        r = NA_QROWS * p + qr
        rs = np.clip(r - NA_WIN_R // 2, 0, rows - NA_WIN_R)
        cs = np.clip(w - NA_WIN_C // 2, 0, GRID_W - NA_WIN_C)
        krow = kb + i
        valid = (krow >= rs) & (krow < rs + NA_WIN_R) & (c >= cs) & (c < cs + NA_WIN_C)
        ok.append(np.broadcast_to(valid, (NA_QROWS, GRID_W, NA_KROWS, GRID_W)))
        rr_sel.append(np.clip(krow - r + NA_WIN_R - 1, 0, n_rr - 1)[:, 0, :, 0])
    ok = np.stack(ok).reshape(len(reps), 1, NA_TQ, NA_TK)
    rr_flat = np.stack(rr_sel).reshape(-1)
    sel = jnp.stack([toe[:, int(rr)] for rr in rr_flat], axis=1)
    sel = sel.reshape(heads, len(reps), NA_QROWS, NA_KROWS, GRID_W, GRID_W)
    sel = sel.transpose(1, 0, 2, 4, 3, 5).reshape(len(reps), heads, NA_TQ, NA_TK)
    return jnp.where(ok, sel, NA_NEG)


def _na_kernel(q_ref, kt_ref, v_ref, kct_ref, vc_ref, *rest):
    bias_refs, o_ref = rest[:-1], rest[-1]
    step = pl.program_id(1)
    n_blocks = kt_ref.shape[1]
    lane = lax.broadcasted_iota(jnp.int32, (NA_TQ, 2 * NA_HEAD_DIM), 1)
    first_head = lane < NA_HEAD_DIM
    n_pairs = len(bias_refs)

    def scores(u):
        pp, h = divmod(u, NA_HEADS)
        jb = jnp.clip(step * n_pairs + pp - (NA_WIN_R // 2) // NA_QROWS, 0, n_blocks - NA_KROWS // NA_QROWS)
        ls = slice((h // 2) * 2 * NA_HEAD_DIM, (h // 2 + 1) * 2 * NA_HEAD_DIM)
        q2 = q_ref[0, pp * NA_TQ:(pp + 1) * NA_TQ, ls]
        qh = jnp.where(first_head if h % 2 == 0 else jnp.logical_not(first_head), q2, jnp.zeros_like(q2))
        kt_win = jnp.concatenate([kt_ref[0, jb + j, ls, :] for j in range(NA_KROWS // NA_QROWS)], axis=1)
        return _dot(qh, kt_win) + bias_refs[pp][0, h], _dot(qh, kct_ref[0, ls, :])

    pending = scores(0)
    tiles = []
    for u in range(n_pairs * NA_HEADS):
        pp, h = divmod(u, NA_HEADS)
        s_w, s_c = pending
        if u + 1 < n_pairs * NA_HEADS:
            pending = scores(u + 1)
        m = jnp.maximum(jnp.max(s_w, axis=-1, keepdims=True), jnp.max(s_c, axis=-1, keepdims=True))
        e_w = jnp.exp2(s_w - m).astype(_BF16)
        e_c = jnp.exp2(s_c - m).astype(_BF16)
        jb = jnp.clip(step * n_pairs + pp - (NA_WIN_R // 2) // NA_QROWS, 0, n_blocks - NA_KROWS // NA_QROWS)
        win_rows = pl.ds(pl.multiple_of(jb * NA_TQ, NA_TQ), NA_TK)
        hl = slice(h * NA_VTILE, (h + 1) * NA_VTILE)
        tiles.append(_dot(e_w, v_ref[0, win_rows, hl]) + _dot(e_c, vc_ref[0, :, hl]))
        if h % 2 == 1:
            num = jnp.where(first_head, tiles[u - 1], tiles[u])
            den = jnp.where(first_head, pltpu.roll(tiles[u - 1], NA_HEAD_DIM, 1),
                            pltpu.roll(tiles[u], NA_HEAD_DIM, 1))
            ls = slice((h // 2) * 2 * NA_HEAD_DIM, (h // 2 + 1) * 2 * NA_HEAD_DIM)
            o_ref[0, pp * NA_TQ:(pp + 1) * NA_TQ, ls] = (num / den).astype(_BF16)


def _neighbourhood_attention(q, kt, v, kct, vc, bias):
    nb, t, width = q.shape
    n_blocks = kt.shape[1]
    lc = vc.shape[1]
    pairs = t // NA_TQ
    n_types = bias.shape[0]

    def bias_spec(pp):
        def index(b, s):
            p = s * NA_STEP_PAIRS + pp
            edge = (n_types - 1) // 2
            kind = jnp.where(p < edge, p, jnp.where(p >= pairs - edge, p - (pairs - n_types), edge))
            return (kind, 0, 0, 0)
        return pl.BlockSpec((1,) + bias.shape[1:], index)

    tq = NA_STEP_PAIRS * NA_TQ
    vmem = (2 * (_nbytes((t, width), _BF16) + _nbytes(v.shape[1:], _BF16)
                 + NA_STEP_PAIRS * _nbytes(bias.shape[1:], _F32)) + 16 * 1024 * 1024)
    return pl.pallas_call(
        _na_kernel,
        grid=(nb, pairs // NA_STEP_PAIRS),
        in_specs=[
            pl.BlockSpec((1, tq, width), lambda b, s: (b, s, 0)),
            pl.BlockSpec((1, n_blocks, width, NA_TQ), lambda b, s: (b, 0, 0, 0)),
            pl.BlockSpec((1,) + v.shape[1:], lambda b, s: (b, 0, 0)),
            pl.BlockSpec((1, width, lc), lambda b, s: (b, 0, 0)),
            pl.BlockSpec((1,) + vc.shape[1:], lambda b, s: (b, 0, 0)),
        ] + [bias_spec(pp) for pp in range(NA_STEP_PAIRS)],
        out_specs=pl.BlockSpec((1, tq, width), lambda b, s: (b, s, 0)),
        out_shape=jax.ShapeDtypeStruct((nb, t, width), _BF16),
        compiler_params=_params(2, vmem),
        name="na_attention",
    )(q, kt, v, kct, vc, *([bias] * NA_STEP_PAIRS))


def _odd_kernel(x_ref, xp_ref, xn_ref, mod_ref, g_ref, win_ref, cw_ref, wo_ref, o_ref):
    t = pl.program_id(1)
    nt = pl.num_programs(1)
    x = x_ref[0]
    tm = x.shape[0]
    n = tm + 2 * ODD_HALO
    xa = jnp.concatenate([xp_ref[0], x, xn_ref[0]], axis=0)
    ha = _pre(xa, mod_ref, g_ref[0:1, :]).astype(_BF16)
    d = x.shape[1]
    cz = _dot(ha, win_ref[0, :, d:2 * d]) * _dot(ha, win_ref[0, :, 2 * d:])
    row = lax.broadcasted_iota(jnp.int32, (n, 1), 0)
    inside = jnp.logical_and(jnp.logical_or(row >= ODD_HALO, t > 0),
                             jnp.logical_or(row < tm + ODD_HALO, t < nt - 1))
    cz = jnp.where(inside, cz, 0.0)
    mid = slice(ODD_HALO, ODD_HALO + tm)
    y = (cw_ref[0:1, :] * pltpu.roll(cz, 1, 0)[mid]
         + cw_ref[1:2, :] * cz[mid]
         + cw_ref[2:3, :] * pltpu.roll(cz, n - 1, 0)[mid])
    z = (_dot(ha[mid], win_ref[0, :, :d]) * y).astype(_BF16)
    out = _dot(z, wo_ref[0])
    o_ref[0] = x + _post(out, mod_ref, g_ref[1:2, :])


def _odd_mixer(x, mod, g, w_in, cw, wo):
    nb, t, d = x.shape
    tm = PROJ_TM
    hb = tm // ODD_HALO
    nh = t // ODD_HALO
    wspec = lambda w: pl.BlockSpec(w.shape, lambda b, i: (0,) * w.ndim)
    vmem = (2 * (_nbytes(w_in.shape, _BF16) + _nbytes(wo.shape, _BF16))
            + 12 * _nbytes((tm + 2 * ODD_HALO, d), _F32) + 8 * 1024 * 1024)
    return pl.pallas_call(
        _odd_kernel,
        grid=(nb, t // tm),
        in_specs=[
            pl.BlockSpec((1, tm, d), lambda b, i: (b, i, 0)),
            pl.BlockSpec((1, ODD_HALO, d), lambda b, i: (b, jnp.maximum(i * hb - 1, 0), 0)),
            pl.BlockSpec((1, ODD_HALO, d), lambda b, i: (b, jnp.minimum((i + 1) * hb, nh - 1), 0)),
            pl.BlockSpec((1, 3, d), lambda b, i: (b, 0, 0)),
            pl.BlockSpec((2, d), lambda b, i: (0, 0)),
            wspec(w_in), wspec(cw), wspec(wo),
        ],
        out_specs=pl.BlockSpec((1, tm, d), lambda b, i: (b, i, 0)),
        out_shape=jax.ShapeDtypeStruct(x.shape, _F32),
        compiler_params=_params(2, vmem),
        name="odd_mixer",
    )(x, x, x, mod, g, w_in, cw, wo)


def kernel(x, c, ctx, c_ctx, w_mod, b_mod, norm_g, ff1_w_gu, ff1_w_down, ff2_w_gu, ff2_w_down,
           ev_w_in, ev_w_out, ev_dw_w, ev_dw_b, ev_ln_g, ev_ln_b, ev_rpb, od_w_in, od_conv_w, od_w_out):
    bsz, seq, d = x.shape
    lc = ctx.shape[1]
    ctx_row = bsz
    cond = jnp.concatenate([c, c_ctx[None, :], jnp.zeros((MOD_ROWS - bsz - 1, d), _F32)], axis=0)
    mods = _ada_mod(cond, w_mod, b_mod).reshape(w_mod.shape[0], MOD_ROWS, N_MOD, d)

    ff1_gu, ff1_dn = _to_bf16(ff1_w_gu), _to_bf16(ff1_w_down)
    ff2_gu, ff2_dn = _to_bf16(ff2_w_gu), _to_bf16(ff2_w_down)
    ev_in, ev_out = _to_bf16(ev_w_in), _to_bf16(ev_w_out)
    od_in, od_out = _to_bf16(od_w_in), _to_bf16(od_w_out)

    m, g = mods[0], norm_g[0]
    x = _ffn(x, m[:, 0:3], g[0:2], ff1_gu, ff1_dn, 0)
    x_ctx = _ffn(ctx.reshape(1, bsz * lc, d), m[:, 0:3], g[0:2], ff1_gu, ff1_dn, 0, shared_row=ctx_row)
    x_ctx = x_ctx.reshape(bsz, lc, d)

    wkt = ev_in[0, :, 3 * CONV_CH:4 * CONV_CH].T
    glu, q, kt, v = _even_in(x, m[:, 3:6], g[2:3], ev_in[:1], wkt)
    kct, vc = _ctx_kv(x_ctx, m[:, 3:6], g[2:3], ev_in[:1], wkt, shared_row=ctx_row)
    ya = _conformer_conv(glu, ev_dw_w[0], ev_dw_b[0], ev_ln_g[0], ev_ln_b[0])
    yb = _neighbourhood_attention(q, kt, v, kct, vc, _na_bias_table(ev_rpb[0]))
    x = _ffn(x, m[:, 6:9], g[4:6], ff2_gu, ff2_dn, 0, mixer=(ya, yb, m[:, 3:6], g[3:4], ev_out[:1]))

    m, g = mods[1], norm_g[1]
    x = _ffn(x, m[:, 0:3], g[0:2], ff1_gu, ff1_dn, 1)
    x = _odd_mixer(x, m[:, 3:6], g[2:4], od_in[:1], od_conv_w[0], od_out[:1])
    x = _ffn(x, m[:, 6:9], g[4:6], ff2_gu, ff2_dn, 1)
    return x
```

```python
import functools
import math

import jax
import jax.numpy as jnp
import numpy as np
from jax import lax
from jax.experimental import pallas as pl
from jax.experimental.pallas import tpu as pltpu

D_MODEL = 1024
D_FF = 2816
N_MOD = 9
EPS = 1e-6
FFN_RES = 0.5
GRID_W = 64
CONV_CH = 512
CONV_GROUPS = 8
CONV_WIDTH = 31
NA_HEADS = 8
NA_HEAD_DIM = 64
NA_WIDTH = NA_HEADS * NA_HEAD_DIM
NA_WIN_R = 8
NA_WIN_C = 16
SC_CONV = 3

V7X_LANES = 128
V7X_SUBLANES = 8
V7X_VMEM_BYTES = 64 * 1024 * 1024
BF16_SUBLANES = 16

MOD_ROWS = 16
MOD_TN = 1536
CAST_BLOCK_BYTES = 6 * 1024 * 1024
FFN_TM = 1024
FFN_SUB = 512
FFN_PIECES = 8
FFN_FC = 256
FFN_NCH = D_FF // FFN_FC
PROJ_TM = 512
EVEN_TM = 1024
CONV_TC = 512
CONV_HALO = 16
CONV_RC = 64
CONV_LC = 256
CONV_NORM_ROWS = 128
NA_QROWS = 2
NA_TQ = NA_QROWS * GRID_W
NA_KROWS = 10
NA_TK = NA_KROWS * GRID_W
NA_NEG = -1e30
LOG2E = math.log2(math.e)
NA_VTILE = 2 * NA_HEAD_DIM
NA_STEP_PAIRS = 4
ODD_HALO = 8

_BF16 = jnp.bfloat16
_F32 = jnp.float32


def _dot(a, b):
    return jnp.dot(a, b, preferred_element_type=_F32)


def _dot_nt(a, b):
    return lax.dot_general(a, b, (((1,), (1,)), ((), ())), preferred_element_type=_F32)


def _silu(x):
    return x * jax.nn.sigmoid(x)


def _rms(x, row):
    return x * lax.rsqrt(jnp.mean(x * x, axis=-1, keepdims=True) + EPS) * row


def _pre(x, mod_ref, g):
    return _rms(x, g * (1.0 + mod_ref[0, 1:2, :])) + mod_ref[0, 0:1, :]


def _post(y, mod_ref, g, weight=1.0):
    return _rms(y, (weight * mod_ref[0, 2:3, :]) * g)


def _params(n_axes, vmem_bytes):
    limit = min(int(vmem_bytes), V7X_VMEM_BYTES - 4 * 1024 * 1024)
    return pltpu.CompilerParams(dimension_semantics=("arbitrary",) * n_axes, vmem_limit_bytes=limit)


def _nbytes(shape, dtype):
    return int(np.prod(shape)) * jnp.dtype(dtype).itemsize


def _mod_kernel(cond_ref, w_ref, b_ref, o_ref):
    a = _silu(cond_ref[...]).astype(_BF16)
    o_ref[0] = _dot(a, w_ref[0].astype(_BF16)) + b_ref[0]


def _ada_mod(cond, w_mod, b_mod):
    depth, d, n = w_mod.shape
    vmem = 2 * _nbytes((d, MOD_TN), _F32) + _nbytes((d, MOD_TN), _BF16) + 8 * 1024 * 1024
    return pl.pallas_call(
        _mod_kernel,
        grid=(depth, n // MOD_TN),
        in_specs=[
            pl.BlockSpec((MOD_ROWS, d), lambda i, j: (0, 0)),
            pl.BlockSpec((1, d, MOD_TN), lambda i, j: (i, 0, j)),
            pl.BlockSpec((1, 1, MOD_TN), lambda i, j: (i, 0, j)),
        ],
        out_specs=pl.BlockSpec((1, MOD_ROWS, MOD_TN), lambda i, j: (i, 0, j)),
        out_shape=jax.ShapeDtypeStruct((depth, MOD_ROWS, n), _F32),
        compiler_params=_params(2, vmem),
        name="ada_mod",
    )(cond, w_mod, b_mod.reshape(depth, 1, n))


def _ffn_kernel(*refs, mixer_out):
    if mixer_out:
        (x_ref, mod_ref, g_ref, wgu_ref, wd_ref, ya_ref, yb_ref, mmod_ref, mg_ref, wo_ref,
         o_ref, h_ref, acc_ref, xin_ref) = refs
    else:
        x_ref, mod_ref, g_ref, wgu_ref, wd_ref, o_ref, h_ref, acc_ref = refs
        xin_ref = x_ref.at[0]
    tm = x_ref.shape[1]
    sub = min(FFN_SUB, tm)
    subs = [slice(s * sub, (s + 1) * sub) for s in range(tm // sub)]
    if mixer_out:
        for rows in subs:
            y = (_dot(ya_ref[0, rows, :], wo_ref[0, :CONV_CH, :])
                 + _dot(yb_ref[0, rows, :], wo_ref[0, CONV_CH:, :]))
            xin_ref[rows, :] = x_ref[0, rows, :] + _post(y, mmod_ref, mg_ref[...])
    always = pl.program_id(0) >= 0
    pieces = min(FFN_PIECES, FFN_NCH)
    piece = sub // pieces

    def paced(value, anchor):
        return value if anchor is None else jnp.where(always, value, anchor[0:1, 0:1])

    def pre(rows, anchor=None):
        h_ref[rows, :] = _pre(paced(xin_ref[rows, :], anchor), mod_ref, g_ref[0:1, :]).astype(_BF16)

    def post(rows, anchor=None):
        y = _post(paced(acc_ref[rows, :], anchor), mod_ref, g_ref[1:2, :], FFN_RES)
        o_ref[0, rows, :] = xin_ref[rows, :] + y

    def piece_rows(s, k):
        return slice(s * sub + k * piece, s * sub + (k + 1) * piece)

    pre(subs[0])
    for s, rows in enumerate(subs):
        for c in range(FFN_NCH):
            h = h_ref[rows, :]
            gate = _dot(h, wgu_ref[0, :, c * FFN_FC:(c + 1) * FFN_FC])
            up = _dot(h, wgu_ref[0, :, D_FF + c * FFN_FC:D_FF + (c + 1) * FFN_FC])
            act = (_silu(gate) * up).astype(_BF16)
            part = _dot(act, wd_ref[0, c * FFN_FC:(c + 1) * FFN_FC, :])
            if c == 0:
                acc_ref[rows, :] = part
            else:
                acc_ref[rows, :] += part
            if c < pieces:
                if s + 1 < len(subs):
                    pre(piece_rows(s + 1, c), anchor=part)
                if s > 0:
                    post(piece_rows(s - 1, c), anchor=part)
    post(subs[-1])


def _ffn(x, mod, g, wgu, wd, layer, shared_row=None, mixer=None):
    nb, t, d = x.shape
    tm = min(FFN_TM, t)
    mod_map = (lambda b, i: (b, 0, 0)) if shared_row is None else (lambda b, i: (shared_row, 0, 0))
    resident = pl.Buffered(1)
    wspec = lambda w: pl.BlockSpec((1,) + w.shape[1:], lambda b, i: (layer, 0, 0), pipeline_mode=resident)
    tok = lambda width: pl.BlockSpec((1, tm, width), lambda b, i: (b, i, 0))
    in_specs = [tok(d), pl.BlockSpec((1, 3, d), mod_map), pl.BlockSpec((2, d), lambda b, i: (0, 0)),
                wspec(wgu), wspec(wd)]
    args = [x, mod, g, wgu, wd]
    scratch = [pltpu.VMEM((tm, d), _BF16), pltpu.VMEM((tm, d), _F32)]
    vmem = (_nbytes(wgu.shape[1:], _BF16) + _nbytes(wd.shape[1:], _BF16) + 4 * _nbytes((tm, d), _F32)
            + _nbytes((tm, d), _BF16) + _nbytes((tm, d), _F32) + 8 * 1024 * 1024)
    if mixer is not None:
        ya, yb, mmod, mg, wo = mixer
        in_specs += [tok(ya.shape[-1]), tok(yb.shape[-1]), pl.BlockSpec((1, 3, d), lambda b, i: (b, 0, 0)),
                     pl.BlockSpec((1, d), lambda b, i: (0, 0)),
                     pl.BlockSpec((1,) + wo.shape[1:], lambda b, i: (0, 0, 0), pipeline_mode=resident)]
        args += [ya, yb, mmod, mg, wo]
        scratch.append(pltpu.VMEM((tm, d), _F32))
        vmem += (_nbytes(wo.shape[1:], _BF16) + _nbytes((tm, d), _F32)
                 + 2 * _nbytes((tm, ya.shape[-1] + yb.shape[-1]), _BF16))
    return pl.pallas_call(
        functools.partial(_ffn_kernel, mixer_out=mixer is not None),
        grid=(nb, t // tm),
        in_specs=in_specs,
        out_specs=tok(d),
        out_shape=jax.ShapeDtypeStruct(x.shape, _F32),
        scratch_shapes=scratch,
        compiler_params=_params(2, vmem),
        name="ffn",
    )(*args)


def _cast_kernel(w_ref, o_ref):
    o_ref[...] = w_ref[...].astype(_BF16)


def _cast_rows(r, c):
    limit = CAST_BLOCK_BYTES // (c * jnp.dtype(_F32).itemsize)
    return max(tr for tr in range(BF16_SUBLANES, r + 1, BF16_SUBLANES) if r % tr == 0 and tr <= limit)


def _to_bf16(w):
    layers, r, c = w.shape
    tr = _cast_rows(r, c)
    spec = pl.BlockSpec((1, tr, c), lambda l, i: (l, i, 0))
    return pl.pallas_call(
        _cast_kernel,
        grid=(layers, r // tr),
        in_specs=[spec],
        out_specs=spec,
        out_shape=jax.ShapeDtypeStruct(w.shape, _BF16),
        compiler_params=_params(2, 4 * CAST_BLOCK_BYTES + 8 * 1024 * 1024),
        name="to_bf16",
    )(w)


def _store_values_with_ones(v, v_ref):
    lane = lax.broadcasted_iota(jnp.int32, (v.shape[0], NA_VTILE), 1)
    first_head = lane < NA_HEAD_DIM
    for j in range(NA_HEADS // 2):
        pair = v[:, j * NA_VTILE:(j + 1) * NA_VTILE]
        v_ref[0, :, 2 * j * NA_VTILE:(2 * j + 1) * NA_VTILE] = jnp.where(first_head, pair, 1.0).astype(_BF16)
        v_ref[0, :, (2 * j + 1) * NA_VTILE:(2 * j + 2) * NA_VTILE] = jnp.where(first_head, 1.0, pair).astype(_BF16)


def _even_in_kernel(x_ref, mod_ref, g_ref, win_ref, wkt_ref, glu_ref, q_ref, kt_ref, v_ref):
    col = lambda j: slice(j * CONV_CH, (j + 1) * CONV_CH)
    h = _pre(x_ref[0], mod_ref, g_ref[...]).astype(_BF16)
    glu_ref[0] = _dot(h, win_ref[0, :, col(0)]) * jax.nn.sigmoid(_dot(h, win_ref[0, :, col(1)]))
    q_ref[0] = (_dot(h, win_ref[0, :, col(2)]) * (NA_HEAD_DIM ** -0.5 * LOG2E)).astype(_BF16)
    _store_values_with_ones(_dot(h, win_ref[0, :, col(4)]), v_ref)
    kt = _dot_nt(wkt_ref[...], h)
    for j in range(kt.shape[1] // NA_TQ):
        kt_ref[0, j] = kt[:, j * NA_TQ:(j + 1) * NA_TQ].astype(_BF16)


def _even_in(x, mod, g, w_in, wkt):
    nb, t, d = x.shape
    tm = EVEN_TM
    vmem = 2 * _nbytes(w_in.shape, _BF16) + 8 * _nbytes((tm, d), _F32) + 16 * 1024 * 1024
    return pl.pallas_call(
        _even_in_kernel,
        grid=(nb, t // tm),
        in_specs=[
            pl.BlockSpec((1, tm, d), lambda b, i: (b, i, 0)),
            pl.BlockSpec((1, 3, d), lambda b, i: (b, 0, 0)),
            pl.BlockSpec((1, d), lambda b, i: (0, 0)),
            pl.BlockSpec(w_in.shape, lambda b, i: (0, 0, 0)),
            pl.BlockSpec(wkt.shape, lambda b, i: (0, 0)),
        ],
        out_specs=[
            pl.BlockSpec((1, tm, CONV_CH), lambda b, i: (b, i, 0)),
            pl.BlockSpec((1, tm, NA_WIDTH), lambda b, i: (b, i, 0)),
            pl.BlockSpec((1, tm // NA_TQ, NA_WIDTH, NA_TQ), lambda b, i: (b, i, 0, 0)),
            pl.BlockSpec((1, tm, NA_HEADS * NA_VTILE), lambda b, i: (b, i, 0)),
        ],
        out_shape=[
            jax.ShapeDtypeStruct((nb, t, CONV_CH), _F32),
            jax.ShapeDtypeStruct((nb, t, NA_WIDTH), _BF16),
            jax.ShapeDtypeStruct((nb, t // NA_TQ, NA_WIDTH, NA_TQ), _BF16),
            jax.ShapeDtypeStruct((nb, t, NA_HEADS * NA_VTILE), _BF16),
        ],
        compiler_params=_params(2, vmem),
        name="even_in",
    )(x, mod, g, w_in, wkt)


def _ctx_kv_kernel(x_ref, mod_ref, g_ref, win_ref, wkt_ref, kt_ref, v_ref):
    h = _pre(x_ref[0], mod_ref, g_ref[...]).astype(_BF16)
    kt_ref[0] = _dot_nt(wkt_ref[...], h).astype(_BF16)
    _store_values_with_ones(_dot(h, win_ref[0, :, 4 * CONV_CH:]), v_ref)


def _ctx_kv(x, mod, g, w_in, wkt, shared_row):
    nb, t, d = x.shape
    return pl.pallas_call(
        _ctx_kv_kernel,
        grid=(nb,),
        in_specs=[
            pl.BlockSpec((1, t, d), lambda b: (b, 0, 0)),
            pl.BlockSpec((1, 3, d), lambda b: (shared_row, 0, 0)),
            pl.BlockSpec((1, d), lambda b: (0, 0)),
            pl.BlockSpec(w_in.shape, lambda b: (0, 0, 0)),
            pl.BlockSpec(wkt.shape, lambda b: (0, 0)),
        ],
        out_specs=[
            pl.BlockSpec((1, NA_WIDTH, t), lambda b: (b, 0, 0)),
            pl.BlockSpec((1, t, NA_HEADS * NA_VTILE), lambda b: (b, 0, 0)),
        ],
        out_shape=[
            jax.ShapeDtypeStruct((nb, NA_WIDTH, t), _BF16),
            jax.ShapeDtypeStruct((nb, t, NA_HEADS * NA_VTILE), _BF16),
        ],
        compiler_params=_params(1, 24 * 1024 * 1024),
        name="ctx_kv",
    )(x, mod, g, w_in, wkt)


def _conv_kernel(main_ref, prev_ref, next_ref, w_ref, b_ref, lng_ref, lnb_ref, grp_ref, o_ref,
                 buf_ref, cv_ref):
    t = pl.program_id(1)
    nt = pl.num_programs(1)
    tc = main_ref.shape[1]
    buf_ref[0:CONV_HALO, :] = jnp.where(t > 0, prev_ref[0], 0.0)
    buf_ref[CONV_HALO:CONV_HALO + tc, :] = main_ref[0]
    buf_ref[CONV_HALO + tc:, :] = jnp.where(t < nt - 1, next_ref[0], 0.0)
    base = CONV_HALO - CONV_WIDTH // 2

    def conv_chunk(r0):
        for lc in range(CONV_CH // CONV_LC):
            lanes = slice(lc * CONV_LC, (lc + 1) * CONV_LC)
            acc = None
            for s in range(V7X_SUBLANES):
                part = None
                for a in range((base + CONV_WIDTH - 1) // V7X_SUBLANES + 1):
                    k = V7X_SUBLANES * a + s - base
                    if 0 <= k < CONV_WIDTH:
                        win = buf_ref[r0 + V7X_SUBLANES * a:r0 + V7X_SUBLANES * a + CONV_RC + V7X_SUBLANES, lanes]
                        term = w_ref[k:k + 1, lanes] * win
                        part = term if part is None else part + term
                shifted = part[s:s + CONV_RC, :]
                acc = shifted if acc is None else acc + shifted
            cv_ref[r0:r0 + CONV_RC, lanes] = acc + b_ref[:, lanes]

    grp = grp_ref[...]
    inv_n = 1.0 / (CONV_CH // CONV_GROUPS)

    def group_mean(z):
        hi = z.astype(_BF16)
        lo = (z - hi.astype(_F32)).astype(_BF16)
        return (_dot(hi, grp) + _dot(lo, grp)) * inv_n

    def group_norm(rows):
        v = cv_ref[rows, :]
        dlt = v - group_mean(v)
        var = _dot((dlt * dlt).astype(_BF16), grp) * inv_n
        vn = dlt * lax.rsqrt(var + EPS) * lng_ref[...] + lnb_ref[...]
        o_ref[0, rows, :] = _silu(vn).astype(_BF16)

    for blk in range(tc // CONV_NORM_ROWS):
        for r0 in range(blk * CONV_NORM_ROWS, (blk + 1) * CONV_NORM_ROWS, CONV_RC):
            conv_chunk(r0)
        group_norm(slice(blk * CONV_NORM_ROWS, (blk + 1) * CONV_NORM_ROWS))


def _conformer_conv(glu, dw_w, dw_b, ln_g, ln_b):
    nb, t, ch = glu.shape
    tc = CONV_TC
    hb = tc // CONV_HALO
    nh = t // CONV_HALO
    gid = np.arange(ch) // (ch // CONV_GROUPS)
    grp = jnp.asarray(gid[:, None] == gid[None, :], _BF16)
    row = lambda a: a.reshape(1, ch)
    rspec = pl.BlockSpec((1, ch), lambda b, i: (0, 0))
    return pl.pallas_call(
        _conv_kernel,
        grid=(nb, t // tc),
        in_specs=[
            pl.BlockSpec((1, tc, ch), lambda b, i: (b, i, 0)),
            pl.BlockSpec((1, CONV_HALO, ch), lambda b, i: (b, jnp.maximum(i * hb - 1, 0), 0)),
            pl.BlockSpec((1, CONV_HALO, ch), lambda b, i: (b, jnp.minimum((i + 1) * hb, nh - 1), 0)),
            pl.BlockSpec((CONV_WIDTH, ch), lambda b, i: (0, 0)),
            rspec, rspec, rspec,
            pl.BlockSpec((ch, ch), lambda b, i: (0, 0)),
        ],
        out_specs=pl.BlockSpec((1, tc, ch), lambda b, i: (b, i, 0)),
        out_shape=jax.ShapeDtypeStruct((nb, t, ch), _BF16),
        scratch_shapes=[pltpu.VMEM((tc + 2 * CONV_HALO, ch), _F32), pltpu.VMEM((tc, ch), _F32)],
        compiler_params=_params(2, 32 * 1024 * 1024),
        name="conformer_conv",
    )(glu, glu, glu, dw_w, row(dw_b), row(ln_g), row(ln_b), grp)


def _na_bias_table(rpb):
    rows = GRID_W
    pairs = rows // NA_QROWS
    reps = [0, 1, 2, pairs - 2, pairs - 1]
    heads, n_rr, n_cc = rpb.shape
    off = np.arange(GRID_W)[None, :] - np.arange(GRID_W)[:, None] + NA_WIN_C - 1
    onehot = (off[None] == np.arange(n_cc)[:, None, None]).reshape(n_cc, GRID_W * GRID_W)
    toe = jnp.einsum("hrk,kn->hrn", rpb.astype(_F32), jnp.asarray(onehot * LOG2E, _F32),
                     precision=lax.Precision.HIGHEST).reshape(heads, n_rr, GRID_W, GRID_W)
    qr = np.arange(NA_QROWS)[:, None, None, None]
    w = np.arange(GRID_W)[None, :, None, None]
    i = np.arange(NA_KROWS)[None, None, :, None]
    c = np.arange(GRID_W)[None, None, None, :]
    rr_sel, ok = [], []
    for p in reps:
        kb = np.clip(NA_QROWS * p - NA_WIN_R // 2, 0, rows - NA_KROWS)
        r = NA_QROWS * p + qr
        rs = np.clip(r - NA_WIN_R // 2, 0, rows - NA_WIN_R)
        cs = np.clip(w - NA_WIN_C // 2, 0, GRID_W - NA_WIN_C)
        krow = kb + i
        valid = (krow >= rs) & (krow < rs + NA_WIN_R) & (c >= cs) & (c < cs + NA_WIN_C)
        ok.append(np.broadcast_to(valid, (NA_QROWS, GRID_W, NA_KROWS, GRID_W)))
        rr_sel.append(np.clip(krow - r + NA_WIN_R - 1, 0, n_rr - 1)[:, 0, :, 0])
    ok = np.stack(ok).reshape(len(reps), 1, NA_TQ, NA_TK)
    rr_flat = np.stack(rr_sel).reshape(-1)
    sel = jnp.stack([toe[:, int(rr)] for rr in rr_flat], axis=1)
    sel = sel.reshape(heads, len(reps), NA_QROWS, NA_KROWS, GRID_W, GRID_W)
    sel = sel.transpose(1, 0, 2, 4, 3, 5).reshape(len(reps), heads, NA_TQ, NA_TK)
    return jnp.where(ok, sel, NA_NEG)


def _na_kernel(q_ref, kt_ref, v_ref, kct_ref, vc_ref, *rest):
    bias_refs, o_ref = rest[:-1], rest[-1]
    step = pl.program_id(1)
    n_blocks = kt_ref.shape[1]
    lane = lax.broadcasted_iota(jnp.int32, (NA_TQ, 2 * NA_HEAD_DIM), 1)
    first_head = lane < NA_HEAD_DIM
    n_pairs = len(bias_refs)

    def scores(u):
        pp, h = divmod(u, NA_HEADS)
        jb = jnp.clip(step * n_pairs + pp - (NA_WIN_R // 2) // NA_QROWS, 0, n_blocks - NA_KROWS // NA_QROWS)
        ls = slice((h // 2) * 2 * NA_HEAD_DIM, (h // 2 + 1) * 2 * NA_HEAD_DIM)
        q2 = q_ref[0, pp * NA_TQ:(pp + 1) * NA_TQ, ls]
        qh = jnp.where(first_head if h % 2 == 0 else jnp.logical_not(first_head), q2, jnp.zeros_like(q2))
        kt_win = jnp.concatenate([kt_ref[0, jb + j, ls, :] for j in range(NA_KROWS // NA_QROWS)], axis=1)
        return _dot(qh, kt_win) + bias_refs[pp][0, h], _dot(qh, kct_ref[0, ls, :])

    pending = scores(0)
    tiles = []
    for u in range(n_pairs * NA_HEADS):
        pp, h = divmod(u, NA_HEADS)
        s_w, s_c = pending
        if u + 1 < n_pairs * NA_HEADS:
            pending = scores(u + 1)
        m = jnp.maximum(jnp.max(s_w, axis=-1, keepdims=True), jnp.max(s_c, axis=-1, keepdims=True))
        e_w = jnp.exp2(s_w - m).astype(_BF16)
        e_c = jnp.exp2(s_c - m).astype(_BF16)
        jb = jnp.clip(step * n_pairs + pp - (NA_WIN_R // 2) // NA_QROWS, 0, n_blocks - NA_KROWS // NA_QROWS)
        win_rows = pl.ds(pl.multiple_of(jb * NA_TQ, NA_TQ), NA_TK)
        hl = slice(h * NA_VTILE, (h + 1) * NA_VTILE)
        tiles.append(_dot(e_w, v_ref[0, win_rows, hl]) + _dot(e_c, vc_ref[0, :, hl]))
        if h % 2 == 1:
            num = jnp.where(first_head, tiles[u - 1], tiles[u])
            den = jnp.where(first_head, pltpu.roll(tiles[u - 1], NA_HEAD_DIM, 1),
                            pltpu.roll(tiles[u], NA_HEAD_DIM, 1))
            ls = slice((h // 2) * 2 * NA_HEAD_DIM, (h // 2 + 1) * 2 * NA_HEAD_DIM)
            o_ref[0, pp * NA_TQ:(pp + 1) * NA_TQ, ls] = (num / den).astype(_BF16)


def _neighbourhood_attention(q, kt, v, kct, vc, bias):
    nb, t, width = q.shape
    n_blocks = kt.shape[1]
    lc = vc.shape[1]
    pairs = t // NA_TQ
    n_types = bias.shape[0]

    def bias_spec(pp):
        def index(b, s):
            p = s * NA_STEP_PAIRS + pp
            edge = (n_types - 1) // 2
            kind = jnp.where(p < edge, p, jnp.where(p >= pairs - edge, p - (pairs - n_types), edge))
            return (kind, 0, 0, 0)
        return pl.BlockSpec((1,) + bias.shape[1:], index)

    tq = NA_STEP_PAIRS * NA_TQ
    vmem = (2 * (_nbytes((t, width), _BF16) + _nbytes(v.shape[1:], _BF16)
                 + NA_STEP_PAIRS * _nbytes(bias.shape[1:], _F32)) + 16 * 1024 * 1024)
    return pl.pallas_call(
        _na_kernel,
        grid=(nb, pairs // NA_STEP_PAIRS),
        in_specs=[
            pl.BlockSpec((1, tq, width), lambda b, s: (b, s, 0)),
            pl.BlockSpec((1, n_blocks, width, NA_TQ), lambda b, s: (b, 0, 0, 0)),
            pl.BlockSpec((1,) + v.shape[1:], lambda b, s: (b, 0, 0)),
            pl.BlockSpec((1, width, lc), lambda b, s: (b, 0, 0)),
            pl.BlockSpec((1,) + vc.shape[1:], lambda b, s: (b, 0, 0)),
        ] + [bias_spec(pp) for pp in range(NA_STEP_PAIRS)],
        out_specs=pl.BlockSpec((1, tq, width), lambda b, s: (b, s, 0)),
        out_shape=jax.ShapeDtypeStruct((nb, t, width), _BF16),
        compiler_params=_params(2, vmem),
        name="na_attention",
    )(q, kt, v, kct, vc, *([bias] * NA_STEP_PAIRS))


def _odd_kernel(x_ref, xp_ref, xn_ref, mod_ref, g_ref, win_ref, cw_ref, wo_ref, o_ref):
    t = pl.program_id(1)
    nt = pl.num_programs(1)
    x = x_ref[0]
    tm = x.shape[0]
    n = tm + 2 * ODD_HALO
    xa = jnp.concatenate([xp_ref[0], x, xn_ref[0]], axis=0)
    ha = _pre(xa, mod_ref, g_ref[0:1, :]).astype(_BF16)
    d = x.shape[1]
    cz = _dot(ha, win_ref[0, :, d:2 * d]) * _dot(ha, win_ref[0, :, 2 * d:])
    row = lax.broadcasted_iota(jnp.int32, (n, 1), 0)
    inside = jnp.logical_and(jnp.logical_or(row >= ODD_HALO, t > 0),
                             jnp.logical_or(row < tm + ODD_HALO, t < nt - 1))
    cz = jnp.where(inside, cz, 0.0)
    mid = slice(ODD_HALO, ODD_HALO + tm)
    y = (cw_ref[0:1, :] * pltpu.roll(cz, 1, 0)[mid]
         + cw_ref[1:2, :] * cz[mid]
         + cw_ref[2:3, :] * pltpu.roll(cz, n - 1, 0)[mid])
    z = (_dot(ha[mid], win_ref[0, :, :d]) * y).astype(_BF16)
    out = _dot(z, wo_ref[0])
    o_ref[0] = x + _post(out, mod_ref, g_ref[1:2, :])


def _odd_mixer(x, mod, g, w_in, cw, wo):
    nb, t, d = x.shape
    tm = PROJ_TM
    hb = tm // ODD_HALO
    nh = t // ODD_HALO
    wspec = lambda w: pl.BlockSpec(w.shape, lambda b, i: (0,) * w.ndim)
    vmem = (2 * (_nbytes(w_in.shape, _BF16) + _nbytes(wo.shape, _BF16))
            + 12 * _nbytes((tm + 2 * ODD_HALO, d), _F32) + 8 * 1024 * 1024)
    return pl.pallas_call(
        _odd_kernel,
        grid=(nb, t // tm),
        in_specs=[
            pl.BlockSpec((1, tm, d), lambda b, i: (b, i, 0)),
            pl.BlockSpec((1, ODD_HALO, d), lambda b, i: (b, jnp.maximum(i * hb - 1, 0), 0)),
            pl.BlockSpec((1, ODD_HALO, d), lambda b, i: (b, jnp.minimum((i + 1) * hb, nh - 1), 0)),
            pl.BlockSpec((1, 3, d), lambda b, i: (b, 0, 0)),
            pl.BlockSpec((2, d), lambda b, i: (0, 0)),
            wspec(w_in), wspec(cw), wspec(wo),
        ],
        out_specs=pl.BlockSpec((1, tm, d), lambda b, i: (b, i, 0)),
        out_shape=jax.ShapeDtypeStruct(x.shape, _F32),
        compiler_params=_params(2, vmem),
        name="odd_mixer",
    )(x, x, x, mod, g, w_in, cw, wo)


def kernel(x, c, ctx, c_ctx, w_mod, b_mod, norm_g, ff1_w_gu, ff1_w_down, ff2_w_gu, ff2_w_down,
           ev_w_in, ev_w_out, ev_dw_w, ev_dw_b, ev_ln_g, ev_ln_b, ev_rpb, od_w_in, od_conv_w, od_w_out):
    bsz, seq, d = x.shape
    lc = ctx.shape[1]
    ctx_row = bsz
    cond = jnp.concatenate([c, c_ctx[None, :], jnp.zeros((MOD_ROWS - bsz - 1, d), _F32)], axis=0)
    mods = _ada_mod(cond, w_mod, b_mod).reshape(w_mod.shape[0], MOD_ROWS, N_MOD, d)

    ff1_gu, ff1_dn = _to_bf16(ff1_w_gu), _to_bf16(ff1_w_down)
    ff2_gu, ff2_dn = _to_bf16(ff2_w_gu), _to_bf16(ff2_w_down)
    ev_in, ev_out = _to_bf16(ev_w_in), _to_bf16(ev_w_out)
    od_in, od_out = _to_bf16(od_w_in), _to_bf16(od_w_out)

    m, g = mods[0], norm_g[0]
    x = _ffn(x, m[:, 0:3], g[0:2], ff1_gu, ff1_dn, 0)
    x_ctx = _ffn(ctx.reshape(1, bsz * lc, d), m[:, 0:3], g[0:2], ff1_gu, ff1_dn, 0, shared_row=ctx_row)
    x_ctx = x_ctx.reshape(bsz, lc, d)

    wkt = ev_in[0, :, 3 * CONV_CH:4 * CONV_CH].T
    glu, q, kt, v = _even_in(x, m[:, 3:6], g[2:3], ev_in[:1], wkt)
    kct, vc = _ctx_kv(x_ctx, m[:, 3:6], g[2:3], ev_in[:1], wkt, shared_row=ctx_row)
    ya = _conformer_conv(glu, ev_dw_w[0], ev_dw_b[0], ev_ln_g[0], ev_ln_b[0])
    yb = _neighbourhood_attention(q, kt, v, kct, vc, _na_bias_table(ev_rpb[0]))
    x = _ffn(x, m[:, 6:9], g[4:6], ff2_gu, ff2_dn, 0, mixer=(ya, yb, m[:, 3:6], g[3:4], ev_out[:1]))

    m, g = mods[1], norm_g[1]
    x = _ffn(x, m[:, 0:3], g[0:2], ff1_gu, ff1_dn, 1)
    x = _odd_mixer(x, m[:, 3:6], g[2:4], od_in[:1], od_conv_w[0], od_out[:1])
    x = _ffn(x, m[:, 6:9], g[4:6], ff2_gu, ff2_dn, 1)
    return x
```

```python
import functools
import math

import jax
import jax.numpy as jnp
import numpy as np
from jax import lax
from jax.experimental import pallas as pl
from jax.experimental.pallas import tpu as pltpu

D_MODEL = 1024
D_FF = 2816
N_MOD = 9
EPS = 1e-6
FFN_RES = 0.5
GRID_W = 64
CONV_CH = 512
CONV_GROUPS = 8
CONV_WIDTH = 31
NA_HEADS = 8
NA_HEAD_DIM = 64
NA_WIDTH = NA_HEADS * NA_HEAD_DIM
NA_WIN_R = 8
NA_WIN_C = 16
SC_CONV = 3

V7X_LANES = 128
V7X_SUBLANES = 8
V7X_VMEM_BYTES = 64 * 1024 * 1024
BF16_SUBLANES = 16

MOD_ROWS = 16
MOD_TN = 1536
CAST_BLOCK_BYTES = 6 * 1024 * 1024
FFN_TM = 2048
FFN_SUB = 512
FFN_TM_MIXER = 1024
FFN_PIECES = 8
FFN_FC = 256
FFN_NCH = D_FF // FFN_FC
PROJ_TM = 512
EVEN_TM = 1024
CONV_TC = 1024
CONV_HALO = 16
CONV_RC = 64
CONV_LC = 256
CONV_NORM_ROWS = 128
NA_QROWS = 2
NA_TQ = NA_QROWS * GRID_W
NA_KROWS = 10
NA_TK = NA_KROWS * GRID_W
NA_NEG = -1e30
LOG2E = math.log2(math.e)
NA_VTILE = 2 * NA_HEAD_DIM
NA_STEP_PAIRS = 4
ODD_HALO = 8

_BF16 = jnp.bfloat16
_F32 = jnp.float32


def _dot(a, b):
    return jnp.dot(a, b, preferred_element_type=_F32)


def _dot_nt(a, b):
    return lax.dot_general(a, b, (((1,), (1,)), ((), ())), preferred_element_type=_F32)


def _silu(x):
    return x * jax.nn.sigmoid(x)


def _rms(x, row):
    return x * lax.rsqrt(jnp.mean(x * x, axis=-1, keepdims=True) + EPS) * row


def _pre(x, mod_ref, g):
    return _rms(x, g * (1.0 + mod_ref[0, 1:2, :])) + mod_ref[0, 0:1, :]


def _post(y, mod_ref, g, weight=1.0):
    return _rms(y, (weight * mod_ref[0, 2:3, :]) * g)


def _params(n_axes, vmem_bytes):
    limit = min(int(vmem_bytes), V7X_VMEM_BYTES - 4 * 1024 * 1024)
    return pltpu.CompilerParams(dimension_semantics=("arbitrary",) * n_axes, vmem_limit_bytes=limit)


def _nbytes(shape, dtype):
    return int(np.prod(shape)) * jnp.dtype(dtype).itemsize


def _mod_kernel(cond_ref, w_ref, b_ref, o_ref):
    a = _silu(cond_ref[...]).astype(_BF16)
    o_ref[0] = _dot(a, w_ref[0].astype(_BF16)) + b_ref[0]


def _ada_mod(cond, w_mod, b_mod):
    depth, d, n = w_mod.shape
    vmem = 2 * _nbytes((d, MOD_TN), _F32) + _nbytes((d, MOD_TN), _BF16) + 8 * 1024 * 1024
    return pl.pallas_call(
        _mod_kernel,
        grid=(depth, n // MOD_TN),
        in_specs=[
            pl.BlockSpec((MOD_ROWS, d), lambda i, j: (0, 0)),
            pl.BlockSpec((1, d, MOD_TN), lambda i, j: (i, 0, j)),
            pl.BlockSpec((1, 1, MOD_TN), lambda i, j: (i, 0, j)),
        ],
        out_specs=pl.BlockSpec((1, MOD_ROWS, MOD_TN), lambda i, j: (i, 0, j)),
        out_shape=jax.ShapeDtypeStruct((depth, MOD_ROWS, n), _F32),
        compiler_params=_params(2, vmem),
        name="ada_mod",
    )(cond, w_mod, b_mod.reshape(depth, 1, n))


def _ffn_kernel(*refs, mixer_out):
    if mixer_out:
        (x_ref, mod_ref, g_ref, wgu_ref, wd_ref, ya_ref, yb_ref, mmod_ref, mg_ref, wo_ref,
         o_ref, h_ref, xin_ref) = refs
    else:
        x_ref, mod_ref, g_ref, wgu_ref, wd_ref, o_ref, h_ref = refs
        xin_ref = x_ref.at[0]
    acc_ref = o_ref.at[0]
    tm = x_ref.shape[1]
    sizes = (FFN_SUB,) * (tm // FFN_SUB) if tm % FFN_SUB == 0 else (tm,)
    starts = [sum(sizes[:s]) for s in range(len(sizes))]
    subs = [slice(a, a + n) for a, n in zip(starts, sizes)]
    if mixer_out:
        for rows in subs:
            y = (_dot(ya_ref[0, rows, :], wo_ref[0, :CONV_CH, :])
                 + _dot(yb_ref[0, rows, :], wo_ref[0, CONV_CH:, :]))
            xin_ref[rows, :] = x_ref[0, rows, :] + _post(y, mmod_ref, mg_ref[...])
    always = pl.program_id(0) >= 0
    pieces = min(FFN_PIECES, FFN_NCH)

    def paced(value, anchor):
        return value if anchor is None else jnp.where(always, value, anchor[0:1, 0:1])

    def pre(rows, anchor=None):
        h_ref[rows, :] = _pre(paced(xin_ref[rows, :], anchor), mod_ref, g_ref[0:1, :]).astype(_BF16)

    def post(rows, anchor=None):
        y = _post(paced(acc_ref[rows, :], anchor), mod_ref, g_ref[1:2, :], FFN_RES)
        o_ref[0, rows, :] = xin_ref[rows, :] + y

    def piece_rows(s, k):
        piece = sizes[s] // pieces
        return slice(starts[s] + k * piece, starts[s] + (k + 1) * piece)

    pre(subs[0])
    for s, rows in enumerate(subs):
        for c in range(FFN_NCH):
            h = h_ref[rows, :]
            gate = _dot(h, wgu_ref[0, :, c * FFN_FC:(c + 1) * FFN_FC])
            up = _dot(h, wgu_ref[0, :, D_FF + c * FFN_FC:D_FF + (c + 1) * FFN_FC])
            act = (_silu(gate) * up).astype(_BF16)
            part = _dot(act, wd_ref[0, c * FFN_FC:(c + 1) * FFN_FC, :])
            if c == 0:
                acc_ref[rows, :] = part
            else:
                acc_ref[rows, :] += part
            if c < pieces:
                if s + 1 < len(subs):
                    pre(piece_rows(s + 1, c), anchor=part)
                if s > 0:
                    post(piece_rows(s - 1, c), anchor=part)
    post(subs[-1])


def _ffn(x, mod, g, wgu, wd, layer, shared_row=None, mixer=None):
    nb, t, d = x.shape
    tm = min(FFN_TM if mixer is None else FFN_TM_MIXER, t)
    mod_map = (lambda b, i: (b, 0, 0)) if shared_row is None else (lambda b, i: (shared_row, 0, 0))
    resident = pl.Buffered(1)
    wspec = lambda w: pl.BlockSpec((1,) + w.shape[1:], lambda b, i: (layer, 0, 0), pipeline_mode=resident)
    tok = lambda width: pl.BlockSpec((1, tm, width), lambda b, i: (b, i, 0))
    in_specs = [tok(d), pl.BlockSpec((1, 3, d), mod_map), pl.BlockSpec((2, d), lambda b, i: (0, 0)),
                wspec(wgu), wspec(wd)]
    args = [x, mod, g, wgu, wd]
    scratch = [pltpu.VMEM((tm, d), _BF16)]
    vmem = (_nbytes(wgu.shape[1:], _BF16) + _nbytes(wd.shape[1:], _BF16) + 4 * _nbytes((tm, d), _F32)
            + _nbytes((tm, d), _BF16) + 8 * 1024 * 1024)
    if mixer is not None:
        ya, yb, mmod, mg, wo = mixer
        in_specs += [tok(ya.shape[-1]), tok(yb.shape[-1]), pl.BlockSpec((1, 3, d), lambda b, i: (b, 0, 0)),
                     pl.BlockSpec((1, d), lambda b, i: (0, 0)),
                     pl.BlockSpec((1,) + wo.shape[1:], lambda b, i: (0, 0, 0), pipeline_mode=resident)]
        args += [ya, yb, mmod, mg, wo]
        scratch.append(pltpu.VMEM((tm, d), _F32))
        vmem += (_nbytes(wo.shape[1:], _BF16) + _nbytes((tm, d), _F32)
                 + 2 * _nbytes((tm, ya.shape[-1] + yb.shape[-1]), _BF16))
    return pl.pallas_call(
        functools.partial(_ffn_kernel, mixer_out=mixer is not None),
        grid=(nb, t // tm),
        in_specs=in_specs,
        out_specs=tok(d),
        out_shape=jax.ShapeDtypeStruct(x.shape, _F32),
        scratch_shapes=scratch,
        compiler_params=_params(2, vmem),
        name="ffn",
    )(*args)


def _cast_kernel(w_ref, o_ref):
    o_ref[...] = w_ref[...].astype(_BF16)


def _cast_rows(r, c):
    limit = CAST_BLOCK_BYTES // (c * jnp.dtype(_F32).itemsize)
    return max(tr for tr in range(BF16_SUBLANES, r + 1, BF16_SUBLANES) if r % tr == 0 and tr <= limit)


def _to_bf16(w):
    layers, r, c = w.shape
    tr = _cast_rows(r, c)
    spec = pl.BlockSpec((1, tr, c), lambda l, i: (l, i, 0))
    return pl.pallas_call(
        _cast_kernel,
        grid=(layers, r // tr),
        in_specs=[spec],
        out_specs=spec,
        out_shape=jax.ShapeDtypeStruct(w.shape, _BF16),
        compiler_params=_params(2, 4 * CAST_BLOCK_BYTES + 8 * 1024 * 1024),
        name="to_bf16",
    )(w)


def _store_values_with_ones(v, v_ref):
    lane = lax.broadcasted_iota(jnp.int32, (v.shape[0], NA_VTILE), 1)
    first_head = lane < NA_HEAD_DIM
    for j in range(NA_HEADS // 2):
        pair = v[:, j * NA_VTILE:(j + 1) * NA_VTILE]
        v_ref[0, :, 2 * j * NA_VTILE:(2 * j + 1) * NA_VTILE] = jnp.where(first_head, pair, 1.0).astype(_BF16)
        v_ref[0, :, (2 * j + 1) * NA_VTILE:(2 * j + 2) * NA_VTILE] = jnp.where(first_head, 1.0, pair).astype(_BF16)


def _even_in_kernel(x_ref, mod_ref, g_ref, win_ref, wkt_ref, glu_ref, q_ref, kt_ref, v_ref):
    col = lambda j: slice(j * CONV_CH, (j + 1) * CONV_CH)
    h = _pre(x_ref[0], mod_ref, g_ref[...]).astype(_BF16)
    glu_ref[0] = _dot(h, win_ref[0, :, col(0)]) * jax.nn.sigmoid(_dot(h, win_ref[0, :, col(1)]))
    q_ref[0] = (_dot(h, win_ref[0, :, col(2)]) * (NA_HEAD_DIM ** -0.5 * LOG2E)).astype(_BF16)
    _store_values_with_ones(_dot(h, win_ref[0, :, col(4)]), v_ref)
    kt_ref[0] = _dot_nt(wkt_ref[...], h).astype(_BF16)


def _even_in(x, mod, g, w_in, wkt):
    nb, t, d = x.shape
    tm = EVEN_TM
    vmem = 2 * _nbytes(w_in.shape, _BF16) + 8 * _nbytes((tm, d), _F32) + 16 * 1024 * 1024
    return pl.pallas_call(
        _even_in_kernel,
        grid=(nb, t // tm),
        in_specs=[
            pl.BlockSpec((1, tm, d), lambda b, i: (b, i, 0)),
            pl.BlockSpec((1, 3, d), lambda b, i: (b, 0, 0)),
            pl.BlockSpec((1, d), lambda b, i: (0, 0)),
            pl.BlockSpec(w_in.shape, lambda b, i: (0, 0, 0)),
            pl.BlockSpec(wkt.shape, lambda b, i: (0, 0)),
        ],
        out_specs=[
            pl.BlockSpec((1, tm, CONV_CH), lambda b, i: (b, i, 0)),
            pl.BlockSpec((1, tm, NA_WIDTH), lambda b, i: (b, i, 0)),
            pl.BlockSpec((1, NA_WIDTH, tm), lambda b, i: (b, 0, i)),
            pl.BlockSpec((1, tm, NA_HEADS * NA_VTILE), lambda b, i: (b, i, 0)),
        ],
        out_shape=[
            jax.ShapeDtypeStruct((nb, t, CONV_CH), _F32),
            jax.ShapeDtypeStruct((nb, t, NA_WIDTH), _BF16),
            jax.ShapeDtypeStruct((nb, NA_WIDTH, t), _BF16),
            jax.ShapeDtypeStruct((nb, t, NA_HEADS * NA_VTILE), _BF16),
        ],
        compiler_params=_params(2, vmem),
        name="even_in",
    )(x, mod, g, w_in, wkt)


def _ctx_kv_kernel(x_ref, mod_ref, g_ref, win_ref, wkt_ref, kt_ref, v_ref):
    h = _pre(x_ref[0], mod_ref, g_ref[...]).astype(_BF16)
    kt_ref[0] = _dot_nt(wkt_ref[...], h).astype(_BF16)
    _store_values_with_ones(_dot(h, win_ref[0, :, 4 * CONV_CH:]), v_ref)


def _ctx_kv(x, mod, g, w_in, wkt, shared_row):
    nb, t, d = x.shape
    return pl.pallas_call(
        _ctx_kv_kernel,
        grid=(nb,),
        in_specs=[
            pl.BlockSpec((1, t, d), lambda b: (b, 0, 0)),
            pl.BlockSpec((1, 3, d), lambda b: (shared_row, 0, 0)),
            pl.BlockSpec((1, d), lambda b: (0, 0)),
            pl.BlockSpec(w_in.shape, lambda b: (0, 0, 0)),
            pl.BlockSpec(wkt.shape, lambda b: (0, 0)),
        ],
        out_specs=[
            pl.BlockSpec((1, NA_WIDTH, t), lambda b: (b, 0, 0)),
            pl.BlockSpec((1, t, NA_HEADS * NA_VTILE), lambda b: (b, 0, 0)),
        ],
        out_shape=[
            jax.ShapeDtypeStruct((nb, NA_WIDTH, t), _BF16),
            jax.ShapeDtypeStruct((nb, t, NA_HEADS * NA_VTILE), _BF16),
        ],
        compiler_params=_params(1, 24 * 1024 * 1024),
        name="ctx_kv",
    )(x, mod, g, w_in, wkt)


def _conv_kernel(main_ref, prev_ref, next_ref, w_ref, b_ref, lng_ref, lnb_ref, grp_ref, o_ref,
                 buf_ref, cv_ref):
    t = pl.program_id(1)
    nt = pl.num_programs(1)
    tc = main_ref.shape[1]
    buf_ref[0:CONV_HALO, :] = jnp.where(t > 0, prev_ref[0], 0.0)
    buf_ref[CONV_HALO:CONV_HALO + tc, :] = main_ref[0]
    buf_ref[CONV_HALO + tc:, :] = jnp.where(t < nt - 1, next_ref[0], 0.0)
    base = CONV_HALO - CONV_WIDTH // 2

    def conv_chunk(r0):
        for lc in range(CONV_CH // CONV_LC):
            lanes = slice(lc * CONV_LC, (lc + 1) * CONV_LC)
            acc = None
            for s in range(V7X_SUBLANES):
                part = None
                for a in range((base + CONV_WIDTH - 1) // V7X_SUBLANES + 1):
                    k = V7X_SUBLANES * a + s - base
                    if 0 <= k < CONV_WIDTH:
                        win = buf_ref[r0 + V7X_SUBLANES * a:r0 + V7X_SUBLANES * a + CONV_RC + V7X_SUBLANES, lanes]
                        term = w_ref[k:k + 1, lanes] * win
                        part = term if part is None else part + term
                shifted = part[s:s + CONV_RC, :]
                acc = shifted if acc is None else acc + shifted
            cv_ref[r0:r0 + CONV_RC, lanes] = acc + b_ref[:, lanes]

    grp = grp_ref[...]
    inv_n = 1.0 / (CONV_CH // CONV_GROUPS)

    def group_mean(z):
        hi = z.astype(_BF16)
        lo = (z - hi.astype(_F32)).astype(_BF16)
        return (_dot(hi, grp) + _dot(lo, grp)) * inv_n

    def group_norm(rows):
        v = cv_ref[rows, :]
        dlt = v - group_mean(v)
        var = _dot((dlt * dlt).astype(_BF16), grp) * inv_n
        vn = dlt * lax.rsqrt(var + EPS) * lng_ref[...] + lnb_ref[...]
        o_ref[0, rows, :] = _silu(vn).astype(_BF16)

    for blk in range(tc // CONV_NORM_ROWS):
        for r0 in range(blk * CONV_NORM_ROWS, (blk + 1) * CONV_NORM_ROWS, CONV_RC):
            conv_chunk(r0)
        group_norm(slice(blk * CONV_NORM_ROWS, (blk + 1) * CONV_NORM_ROWS))


def _conformer_conv(glu, dw_w, dw_b, ln_g, ln_b):
    nb, t, ch = glu.shape
    tc = CONV_TC
    hb = tc // CONV_HALO
    nh = t // CONV_HALO
    gid = np.arange(ch) // (ch // CONV_GROUPS)
    grp = jnp.asarray(gid[:, None] == gid[None, :], _BF16)
    row = lambda a: a.reshape(1, ch)
    rspec = pl.BlockSpec((1, ch), lambda b, i: (0, 0))
    return pl.pallas_call(
        _conv_kernel,
        grid=(nb, t // tc),
        in_specs=[
            pl.BlockSpec((1, tc, ch), lambda b, i: (b, i, 0)),
            pl.BlockSpec((1, CONV_HALO, ch), lambda b, i: (b, jnp.maximum(i * hb - 1, 0), 0)),
            pl.BlockSpec((1, CONV_HALO, ch), lambda b, i: (b, jnp.minimum((i + 1) * hb, nh - 1), 0)),
            pl.BlockSpec((CONV_WIDTH, ch), lambda b, i: (0, 0)),
            rspec, rspec, rspec,
            pl.BlockSpec((ch, ch), lambda b, i: (0, 0)),
        ],
        out_specs=pl.BlockSpec((1, tc, ch), lambda b, i: (b, i, 0)),
        out_shape=jax.ShapeDtypeStruct((nb, t, ch), _BF16),
        scratch_shapes=[pltpu.VMEM((tc + 2 * CONV_HALO, ch), _F32), pltpu.VMEM((tc, ch), _F32)],
        compiler_params=_params(2, 32 * 1024 * 1024),
        name="conformer_conv",
    )(glu, glu, glu, dw_w, row(dw_b), row(ln_g), row(ln_b), grp)


def _na_bias_table(rpb):
    rows = GRID_W
    pairs = rows // NA_QROWS
    reps = [0, 1, 2, pairs - 2, pairs - 1]
    heads, n_rr, n_cc = rpb.shape
    off = np.arange(GRID_W)[None, :] - np.arange(GRID_W)[:, None] + NA_WIN_C - 1
    onehot = (off[None] == np.arange(n_cc)[:, None, None]).reshape(n_cc, GRID_W * GRID_W)
    toe = jnp.einsum("hrk,kn->hrn", rpb.astype(_F32), jnp.asarray(onehot * LOG2E, _F32),
                     precision=lax.Precision.HIGHEST).reshape(heads, n_rr, GRID_W, GRID_W)
    qr = np.arange(NA_QROWS)[:, None, None, None]
    w = np.arange(GRID_W)[None, :, None, None]
    i = np.arange(NA_KROWS)[None, None, :, None]
    c = np.arange(GRID_W)[None, None, None, :]
    rr_sel, ok = [], []
    for p in reps:
        kb = np.clip(NA_QROWS * p - NA_WIN_R // 2, 0, rows - NA_KROWS)
        r = NA_QROWS * p + qr
        rs = np.clip(r - NA_WIN_R // 2, 0, rows - NA_WIN_R)
        cs = np.clip(w - NA_WIN_C // 2, 0, GRID_W - NA_WIN_C)
        krow = kb + i
        valid = (krow >= rs) & (krow < rs + NA_WIN_R) & (c >= cs) & (c < cs + NA_WIN_C)
        ok.append(np.broadcast_to(valid, (NA_QROWS, GRID_W, NA_KROWS, GRID_W)))
        rr_sel.append(np.clip(krow - r + NA_WIN_R - 1, 0, n_rr - 1)[:, 0, :, 0])
    ok = np.stack(ok).reshape(len(reps), 1, NA_TQ, NA_TK)
    rr_flat = np.stack(rr_sel).reshape(-1)
    sel = jnp.stack([toe[:, int(rr)] for rr in rr_flat], axis=1)
    sel = sel.reshape(heads, len(reps), NA_QROWS, NA_KROWS, GRID_W, GRID_W)
    sel = sel.transpose(1, 0, 2, 4, 3, 5).reshape(len(reps), heads, NA_TQ, NA_TK)
    return jnp.where(ok, sel, NA_NEG)


def _na_kernel(q_ref, kt_ref, v_ref, kct_ref, vc_ref, *rest):
    bias_refs, o_ref = rest[:-1], rest[-1]
    step = pl.program_id(1)
    n_blocks = kt_ref.shape[2] // NA_TQ
    lane = lax.broadcasted_iota(jnp.int32, (NA_TQ, 2 * NA_HEAD_DIM), 1)
    first_head = lane < NA_HEAD_DIM
    n_pairs = len(bias_refs)

    def scores(u):
        pp, h = divmod(u, NA_HEADS)
        jb = jnp.clip(step * n_pairs + pp - (NA_WIN_R // 2) // NA_QROWS, 0, n_blocks - NA_KROWS // NA_QROWS)
        ls = slice((h // 2) * 2 * NA_HEAD_DIM, (h // 2 + 1) * 2 * NA_HEAD_DIM)
        q2 = q_ref[0, pp * NA_TQ:(pp + 1) * NA_TQ, ls]
        qh = jnp.where(first_head if h % 2 == 0 else jnp.logical_not(first_head), q2, jnp.zeros_like(q2))
        kt_win = kt_ref[0, ls, pl.ds(pl.multiple_of(jb * NA_TQ, NA_TQ), NA_TK)]
        return _dot(qh, kt_win) + bias_refs[pp][0, h], _dot(qh, kct_ref[0, ls, :])

    pending = scores(0)
    tiles = []
    for u in range(n_pairs * NA_HEADS):
        pp, h = divmod(u, NA_HEADS)
        s_w, s_c = pending
        if u + 1 < n_pairs * NA_HEADS:
            pending = scores(u + 1)
        m = jnp.maximum(jnp.max(s_w, axis=-1, keepdims=True), jnp.max(s_c, axis=-1, keepdims=True))
        e_w = jnp.exp2(s_w - m).astype(_BF16)
        e_c = jnp.exp2(s_c - m).astype(_BF16)
        jb = jnp.clip(step * n_pairs + pp - (NA_WIN_R // 2) // NA_QROWS, 0, n_blocks - NA_KROWS // NA_QROWS)
        win_rows = pl.ds(pl.multiple_of(jb * NA_TQ, NA_TQ), NA_TK)
        hl = slice(h * NA_VTILE, (h + 1) * NA_VTILE)
        tiles.append(_dot(e_w, v_ref[0, win_rows, hl]) + _dot(e_c, vc_ref[0, :, hl]))
        if h % 2 == 1:
            num = jnp.where(first_head, tiles[u - 1], tiles[u])
            den = jnp.where(first_head, pltpu.roll(tiles[u - 1], NA_HEAD_DIM, 1),
                            pltpu.roll(tiles[u], NA_HEAD_DIM, 1))
            ls = slice((h // 2) * 2 * NA_HEAD_DIM, (h // 2 + 1) * 2 * NA_HEAD_DIM)
            o_ref[0, pp * NA_TQ:(pp + 1) * NA_TQ, ls] = (num / den).astype(_BF16)


def _neighbourhood_attention(q, kt, v, kct, vc, bias):
    nb, t, width = q.shape
    lc = vc.shape[1]
    pairs = t // NA_TQ
    n_types = bias.shape[0]

    def bias_spec(pp):
        def index(b, s):
            p = s * NA_STEP_PAIRS + pp
            edge = (n_types - 1) // 2
            kind = jnp.where(p < edge, p, jnp.where(p >= pairs - edge, p - (pairs - n_types), edge))
            return (kind, 0, 0, 0)
        return pl.BlockSpec((1,) + bias.shape[1:], index)

    tq = NA_STEP_PAIRS * NA_TQ
    vmem = (2 * (_nbytes((t, width), _BF16) + _nbytes(v.shape[1:], _BF16)
                 + NA_STEP_PAIRS * _nbytes(bias.shape[1:], _F32)) + 16 * 1024 * 1024)
    return pl.pallas_call(
        _na_kernel,
        grid=(nb, pairs // NA_STEP_PAIRS),
        in_specs=[
            pl.BlockSpec((1, tq, width), lambda b, s: (b, s, 0)),
            pl.BlockSpec((1, width, t), lambda b, s: (b, 0, 0)),
            pl.BlockSpec((1,) + v.shape[1:], lambda b, s: (b, 0, 0)),
            pl.BlockSpec((1, width, lc), lambda b, s: (b, 0, 0)),
            pl.BlockSpec((1,) + vc.shape[1:], lambda b, s: (b, 0, 0)),
        ] + [bias_spec(pp) for pp in range(NA_STEP_PAIRS)],
        out_specs=pl.BlockSpec((1, tq, width), lambda b, s: (b, s, 0)),
        out_shape=jax.ShapeDtypeStruct((nb, t, width), _BF16),
        compiler_params=_params(2, vmem),
        name="na_attention",
    )(q, kt, v, kct, vc, *([bias] * NA_STEP_PAIRS))


def _odd_kernel(x_ref, xp_ref, xn_ref, mod_ref, g_ref, win_ref, cw_ref, wo_ref, o_ref):
    t = pl.program_id(1)
    nt = pl.num_programs(1)
    x = x_ref[0]
    tm = x.shape[0]
    n = tm + 2 * ODD_HALO
    xa = jnp.concatenate([xp_ref[0], x, xn_ref[0]], axis=0)
    ha = _pre(xa, mod_ref, g_ref[0:1, :]).astype(_BF16)
    d = x.shape[1]
    cz = _dot(ha, win_ref[0, :, d:2 * d]) * _dot(ha, win_ref[0, :, 2 * d:])
    row = lax.broadcasted_iota(jnp.int32, (n, 1), 0)
    inside = jnp.logical_and(jnp.logical_or(row >= ODD_HALO, t > 0),
                             jnp.logical_or(row < tm + ODD_HALO, t < nt - 1))
    cz = jnp.where(inside, cz, 0.0)
    mid = slice(ODD_HALO, ODD_HALO + tm)
    y = (cw_ref[0:1, :] * pltpu.roll(cz, 1, 0)[mid]
         + cw_ref[1:2, :] * cz[mid]
         + cw_ref[2:3, :] * pltpu.roll(cz, n - 1, 0)[mid])
    z = (_dot(ha[mid], win_ref[0, :, :d]) * y).astype(_BF16)
    out = _dot(z, wo_ref[0])
    o_ref[0] = x + _post(out, mod_ref, g_ref[1:2, :])


def _odd_mixer(x, mod, g, w_in, cw, wo):
    nb, t, d = x.shape
    tm = PROJ_TM
    hb = tm // ODD_HALO
    nh = t // ODD_HALO
    wspec = lambda w: pl.BlockSpec(w.shape, lambda b, i: (0,) * w.ndim)
    vmem = (2 * (_nbytes(w_in.shape, _BF16) + _nbytes(wo.shape, _BF16))
            + 12 * _nbytes((tm + 2 * ODD_HALO, d), _F32) + 8 * 1024 * 1024)
    return pl.pallas_call(
        _odd_kernel,
        grid=(nb, t // tm),
        in_specs=[
            pl.BlockSpec((1, tm, d), lambda b, i: (b, i, 0)),
            pl.BlockSpec((1, ODD_HALO, d), lambda b, i: (b, jnp.maximum(i * hb - 1, 0), 0)),
            pl.BlockSpec((1, ODD_HALO, d), lambda b, i: (b, jnp.minimum((i + 1) * hb, nh - 1), 0)),
            pl.BlockSpec((1, 3, d), lambda b, i: (b, 0, 0)),
            pl.BlockSpec((2, d), lambda b, i: (0, 0)),
            wspec(w_in), wspec(cw), wspec(wo),
        ],
        out_specs=pl.BlockSpec((1, tm, d), lambda b, i: (b, i, 0)),
        out_shape=jax.ShapeDtypeStruct(x.shape, _F32),
        compiler_params=_params(2, vmem),
        name="odd_mixer",
    )(x, x, x, mod, g, w_in, cw, wo)


def kernel(x, c, ctx, c_ctx, w_mod, b_mod, norm_g, ff1_w_gu, ff1_w_down, ff2_w_gu, ff2_w_down,
           ev_w_in, ev_w_out, ev_dw_w, ev_dw_b, ev_ln_g, ev_ln_b, ev_rpb, od_w_in, od_conv_w, od_w_out):
    bsz, seq, d = x.shape
    lc = ctx.shape[1]
    ctx_row = bsz
    cond = jnp.concatenate([c, c_ctx[None, :], jnp.zeros((MOD_ROWS - bsz - 1, d), _F32)], axis=0)
    mods = _ada_mod(cond, w_mod, b_mod).reshape(w_mod.shape[0], MOD_ROWS, N_MOD, d)

    ff1_gu, ff1_dn = _to_bf16(ff1_w_gu), _to_bf16(ff1_w_down)
    ff2_gu, ff2_dn = _to_bf16(ff2_w_gu), _to_bf16(ff2_w_down)
    ev_in, ev_out = _to_bf16(ev_w_in), _to_bf16(ev_w_out)
    od_in, od_out = _to_bf16(od_w_in), _to_bf16(od_w_out)

    m, g = mods[0], norm_g[0]
    x = _ffn(x, m[:, 0:3], g[0:2], ff1_gu, ff1_dn, 0)
    x_ctx = _ffn(ctx.reshape(1, bsz * lc, d), m[:, 0:3], g[0:2], ff1_gu, ff1_dn, 0, shared_row=ctx_row)
    x_ctx = x_ctx.reshape(bsz, lc, d)

    wkt = ev_in[0, :, 3 * CONV_CH:4 * CONV_CH].T
    glu, q, kt, v = _even_in(x, m[:, 3:6], g[2:3], ev_in[:1], wkt)
    kct, vc = _ctx_kv(x_ctx, m[:, 3:6], g[2:3], ev_in[:1], wkt, shared_row=ctx_row)
    ya = _conformer_conv(glu, ev_dw_w[0], ev_dw_b[0], ev_ln_g[0], ev_ln_b[0])
    yb = _neighbourhood_attention(q, kt, v, kct, vc, _na_bias_table(ev_rpb[0]))
    x = _ffn(x, m[:, 6:9], g[4:6], ff2_gu, ff2_dn, 0, mixer=(ya, yb, m[:, 3:6], g[3:4], ev_out[:1]))

    m, g = mods[1], norm_g[1]
    x = _ffn(x, m[:, 0:3], g[0:2], ff1_gu, ff1_dn, 1)
    x = _odd_mixer(x, m[:, 3:6], g[2:4], od_in[:1], od_conv_w[0], od_out[:1])
    x = _ffn(x, m[:, 6:9], g[4:6], ff2_gu, ff2_dn, 1)
    return x
```

```python
import functools
import math

import jax
import jax.numpy as jnp
import numpy as np
from jax import lax
from jax.experimental import pallas as pl
from jax.experimental.pallas import tpu as pltpu

D_MODEL = 1024
D_FF = 2816
N_MOD = 9
EPS = 1e-6
FFN_RES = 0.5
GRID_W = 64
CONV_CH = 512
CONV_GROUPS = 8
CONV_WIDTH = 31
NA_HEADS = 8
NA_HEAD_DIM = 64
NA_WIDTH = NA_HEADS * NA_HEAD_DIM
NA_WIN_R = 8
NA_WIN_C = 16
SC_CONV = 3

V7X_LANES = 128
V7X_SUBLANES = 8
V7X_VMEM_BYTES = 64 * 1024 * 1024
BF16_SUBLANES = 16

MOD_ROWS = 16
MOD_TN = 1536
CAST_BLOCK_BYTES = 6 * 1024 * 1024
FFN_TM = 1024
FFN_SUB = 512
FFN_PIECES = 8
FFN_FC = 256
FFN_NCH = D_FF // FFN_FC
PROJ_TM = 512
EVEN_TM = 1024
CONV_TC = 1024
CONV_HALO = 16
CONV_RC = 64
CONV_LC = 256
CONV_NORM_ROWS = 128
NA_QROWS = 2
NA_TQ = NA_QROWS * GRID_W
NA_KROWS = 10
NA_TK = NA_KROWS * GRID_W
NA_NEG = -1e30
LOG2E = math.log2(math.e)
NA_VTILE = 2 * NA_HEAD_DIM
NA_STEP_PAIRS = 4
ODD_HALO = 8

_BF16 = jnp.bfloat16
_F32 = jnp.float32


def _dot(a, b):
    return jnp.dot(a, b, preferred_element_type=_F32)


def _dot_nt(a, b):
    return lax.dot_general(a, b, (((1,), (1,)), ((), ())), preferred_element_type=_F32)


def _silu(x):
    return x * jax.nn.sigmoid(x)


def _rms(x, row):
    return x * lax.rsqrt(jnp.mean(x * x, axis=-1, keepdims=True) + EPS) * row


def _pre(x, mod_ref, g):
    return _rms(x, g * (1.0 + mod_ref[0, 1:2, :])) + mod_ref[0, 0:1, :]


def _post(y, mod_ref, g, weight=1.0):
    return _rms(y, (weight * mod_ref[0, 2:3, :]) * g)


def _params(n_axes, vmem_bytes):
    limit = min(int(vmem_bytes), V7X_VMEM_BYTES - 4 * 1024 * 1024)
    return pltpu.CompilerParams(dimension_semantics=("arbitrary",) * n_axes, vmem_limit_bytes=limit)


def _nbytes(shape, dtype):
    return int(np.prod(shape)) * jnp.dtype(dtype).itemsize


def _mod_kernel(cond_ref, w_ref, b_ref, o_ref):
    a = _silu(cond_ref[...]).astype(_BF16)
    o_ref[0] = _dot(a, w_ref[0].astype(_BF16)) + b_ref[0]


def _ada_mod(cond, w_mod, b_mod):
    depth, d, n = w_mod.shape
    vmem = 2 * _nbytes((d, MOD_TN), _F32) + _nbytes((d, MOD_TN), _BF16) + 8 * 1024 * 1024
    return pl.pallas_call(
        _mod_kernel,
        grid=(depth, n // MOD_TN),
        in_specs=[
            pl.BlockSpec((MOD_ROWS, d), lambda i, j: (0, 0)),
            pl.BlockSpec((1, d, MOD_TN), lambda i, j: (i, 0, j)),
            pl.BlockSpec((1, 1, MOD_TN), lambda i, j: (i, 0, j)),
        ],
        out_specs=pl.BlockSpec((1, MOD_ROWS, MOD_TN), lambda i, j: (i, 0, j)),
        out_shape=jax.ShapeDtypeStruct((depth, MOD_ROWS, n), _F32),
        compiler_params=_params(2, vmem),
        name="ada_mod",
    )(cond, w_mod, b_mod.reshape(depth, 1, n))


def _ffn_kernel(*refs, mixer_out):
    if mixer_out:
        (x_ref, mod_ref, g_ref, wgu_ref, wd_ref, ya_ref, yb_ref, mmod_ref, mg_ref, wo_ref,
         o_ref, h_ref, xin_ref) = refs
    else:
        x_ref, mod_ref, g_ref, wgu_ref, wd_ref, o_ref, h_ref = refs
        xin_ref = x_ref.at[0]
    acc_ref = o_ref.at[0]
    tm = x_ref.shape[1]
    sizes = (FFN_SUB,) * (tm // FFN_SUB) if tm % FFN_SUB == 0 else (tm,)
    starts = [sum(sizes[:s]) for s in range(len(sizes))]
    subs = [slice(a, a + n) for a, n in zip(starts, sizes)]
    if mixer_out:
        for rows in subs:
            y = (_dot(ya_ref[0, rows, :], wo_ref[0, :CONV_CH, :])
                 + _dot(yb_ref[0, rows, :], wo_ref[0, CONV_CH:, :]))
            xin_ref[rows, :] = x_ref[0, rows, :] + _post(y, mmod_ref, mg_ref[...])
    always = pl.program_id(0) >= 0
    pieces = min(FFN_PIECES, FFN_NCH)

    def paced(value, anchor):
        return value if anchor is None else jnp.where(always, value, anchor[0:1, 0:1])

    def pre(rows, anchor=None):
        h_ref[rows, :] = _pre(paced(xin_ref[rows, :], anchor), mod_ref, g_ref[0:1, :]).astype(_BF16)

    def post(rows, anchor=None):
        y = _post(paced(acc_ref[rows, :], anchor), mod_ref, g_ref[1:2, :], FFN_RES)
        o_ref[0, rows, :] = xin_ref[rows, :] + y

    def piece_rows(s, k):
        piece = sizes[s] // pieces
        return slice(starts[s] + k * piece, starts[s] + (k + 1) * piece)

    pre(subs[0])
    for s, rows in enumerate(subs):
        for c in range(FFN_NCH):
            h = h_ref[rows, :]
            gate = _dot(h, wgu_ref[0, :, c * FFN_FC:(c + 1) * FFN_FC])
            up = _dot(h, wgu_ref[0, :, D_FF + c * FFN_FC:D_FF + (c + 1) * FFN_FC])
            act = (_silu(gate) * up).astype(_BF16)
            part = _dot(act, wd_ref[0, c * FFN_FC:(c + 1) * FFN_FC, :])
            if c == 0:
                acc_ref[rows, :] = part
            else:
                acc_ref[rows, :] += part
            if c < pieces:
                if s + 1 < len(subs):
                    pre(piece_rows(s + 1, c), anchor=part)
                if s > 0:
                    post(piece_rows(s - 1, c), anchor=part)
    post(subs[-1])


def _ffn(x, mod, g, wgu, wd, layer, shared_row=None, mixer=None):
    nb, t, d = x.shape
    tm = min(FFN_TM, t)
    mod_map = (lambda b, i: (b, 0, 0)) if shared_row is None else (lambda b, i: (shared_row, 0, 0))
    resident = pl.Buffered(1)
    wspec = lambda w: pl.BlockSpec((1,) + w.shape[1:], lambda b, i: (layer, 0, 0), pipeline_mode=resident)
    tok = lambda width: pl.BlockSpec((1, tm, width), lambda b, i: (b, i, 0))
    in_specs = [tok(d), pl.BlockSpec((1, 3, d), mod_map), pl.BlockSpec((2, d), lambda b, i: (0, 0)),
                wspec(wgu), wspec(wd)]
    args = [x, mod, g, wgu, wd]
    scratch = [pltpu.VMEM((tm, d), _BF16)]
    vmem = (_nbytes(wgu.shape[1:], _BF16) + _nbytes(wd.shape[1:], _BF16) + 4 * _nbytes((tm, d), _F32)
            + _nbytes((tm, d), _BF16) + 8 * 1024 * 1024)
    if mixer is not None:
        ya, yb, mmod, mg, wo = mixer
        in_specs += [tok(ya.shape[-1]), tok(yb.shape[-1]), pl.BlockSpec((1, 3, d), lambda b, i: (b, 0, 0)),
                     pl.BlockSpec((1, d), lambda b, i: (0, 0)),
                     pl.BlockSpec((1,) + wo.shape[1:], lambda b, i: (0, 0, 0), pipeline_mode=resident)]
        args += [ya, yb, mmod, mg, wo]
        scratch.append(pltpu.VMEM((tm, d), _F32))
        vmem += (_nbytes(wo.shape[1:], _BF16) + _nbytes((tm, d), _F32)
                 + 2 * _nbytes((tm, ya.shape[-1] + yb.shape[-1]), _BF16))
    return pl.pallas_call(
        functools.partial(_ffn_kernel, mixer_out=mixer is not None),
        grid=(nb, t // tm),
        in_specs=in_specs,
        out_specs=tok(d),
        out_shape=jax.ShapeDtypeStruct(x.shape, _F32),
        scratch_shapes=scratch,
        compiler_params=_params(2, vmem),
        name="ffn",
    )(*args)


def _cast_kernel(w_ref, o_ref):
    o_ref[...] = w_ref[...].astype(_BF16)


def _cast_rows(r, c):
    limit = CAST_BLOCK_BYTES // (c * jnp.dtype(_F32).itemsize)
    return max(tr for tr in range(BF16_SUBLANES, r + 1, BF16_SUBLANES) if r % tr == 0 and tr <= limit)


def _to_bf16(w):
    layers, r, c = w.shape
    tr = _cast_rows(r, c)
    spec = pl.BlockSpec((1, tr, c), lambda l, i: (l, i, 0))
    return pl.pallas_call(
        _cast_kernel,
        grid=(layers, r // tr),
        in_specs=[spec],
        out_specs=spec,
        out_shape=jax.ShapeDtypeStruct(w.shape, _BF16),
        compiler_params=_params(2, 4 * CAST_BLOCK_BYTES + 8 * 1024 * 1024),
        name="to_bf16",
    )(w)


def _store_values_with_ones(v, v_ref):
    lane = lax.broadcasted_iota(jnp.int32, (v.shape[0], NA_VTILE), 1)
    first_head = lane < NA_HEAD_DIM
    for j in range(NA_HEADS // 2):
        pair = v[:, j * NA_VTILE:(j + 1) * NA_VTILE]
        v_ref[0, :, 2 * j * NA_VTILE:(2 * j + 1) * NA_VTILE] = jnp.where(first_head, pair, 1.0).astype(_BF16)
        v_ref[0, :, (2 * j + 1) * NA_VTILE:(2 * j + 2) * NA_VTILE] = jnp.where(first_head, 1.0, pair).astype(_BF16)


def _even_in_kernel(x_ref, mod_ref, g_ref, win_ref, wkt_ref, glu_ref, q_ref, kt_ref, v_ref):
    col = lambda j: slice(j * CONV_CH, (j + 1) * CONV_CH)
    h = _pre(x_ref[0], mod_ref, g_ref[...]).astype(_BF16)
    glu_ref[0] = _dot(h, win_ref[0, :, col(0)]) * jax.nn.sigmoid(_dot(h, win_ref[0, :, col(1)]))
    q_ref[0] = (_dot(h, win_ref[0, :, col(2)]) * (NA_HEAD_DIM ** -0.5 * LOG2E)).astype(_BF16)
    _store_values_with_ones(_dot(h, win_ref[0, :, col(4)]), v_ref)
    kt = _dot_nt(wkt_ref[...], h)
    for j in range(kt.shape[1] // NA_TQ):
        kt_ref[0, j] = kt[:, j * NA_TQ:(j + 1) * NA_TQ].astype(_BF16)


def _even_in(x, mod, g, w_in, wkt):
    nb, t, d = x.shape
    tm = EVEN_TM
    vmem = 2 * _nbytes(w_in.shape, _BF16) + 8 * _nbytes((tm, d), _F32) + 16 * 1024 * 1024
    return pl.pallas_call(
        _even_in_kernel,
        grid=(nb, t // tm),
        in_specs=[
            pl.BlockSpec((1, tm, d), lambda b, i: (b, i, 0)),
            pl.BlockSpec((1, 3, d), lambda b, i: (b, 0, 0)),
            pl.BlockSpec((1, d), lambda b, i: (0, 0)),
            pl.BlockSpec(w_in.shape, lambda b, i: (0, 0, 0)),
            pl.BlockSpec(wkt.shape, lambda b, i: (0, 0)),
        ],
        out_specs=[
            pl.BlockSpec((1, tm, CONV_CH), lambda b, i: (b, i, 0)),
            pl.BlockSpec((1, tm, NA_WIDTH), lambda b, i: (b, i, 0)),
            pl.BlockSpec((1, tm // NA_TQ, NA_WIDTH, NA_TQ), lambda b, i: (b, i, 0, 0)),
            pl.BlockSpec((1, tm, NA_HEADS * NA_VTILE), lambda b, i: (b, i, 0)),
        ],
        out_shape=[
            jax.ShapeDtypeStruct((nb, t, CONV_CH), _F32),
            jax.ShapeDtypeStruct((nb, t, NA_WIDTH), _BF16),
            jax.ShapeDtypeStruct((nb, t // NA_TQ, NA_WIDTH, NA_TQ), _BF16),
            jax.ShapeDtypeStruct((nb, t, NA_HEADS * NA_VTILE), _BF16),
        ],
        compiler_params=_params(2, vmem),
        name="even_in",
    )(x, mod, g, w_in, wkt)


def _ctx_kv_kernel(x_ref, mod_ref, g_ref, win_ref, wkt_ref, kt_ref, v_ref):
    h = _pre(x_ref[0], mod_ref, g_ref[...]).astype(_BF16)
    kt_ref[0] = _dot_nt(wkt_ref[...], h).astype(_BF16)
    _store_values_with_ones(_dot(h, win_ref[0, :, 4 * CONV_CH:]), v_ref)


def _ctx_kv(x, mod, g, w_in, wkt, shared_row):
    nb, t, d = x.shape
    return pl.pallas_call(
        _ctx_kv_kernel,
        grid=(nb,),
        in_specs=[
            pl.BlockSpec((1, t, d), lambda b: (b, 0, 0)),
            pl.BlockSpec((1, 3, d), lambda b: (shared_row, 0, 0)),
            pl.BlockSpec((1, d), lambda b: (0, 0)),
            pl.BlockSpec(w_in.shape, lambda b: (0, 0, 0)),
            pl.BlockSpec(wkt.shape, lambda b: (0, 0)),
        ],
        out_specs=[
            pl.BlockSpec((1, NA_WIDTH, t), lambda b: (b, 0, 0)),
            pl.BlockSpec((1, t, NA_HEADS * NA_VTILE), lambda b: (b, 0, 0)),
        ],
        out_shape=[
            jax.ShapeDtypeStruct((nb, NA_WIDTH, t), _BF16),
            jax.ShapeDtypeStruct((nb, t, NA_HEADS * NA_VTILE), _BF16),
        ],
        compiler_params=_params(1, 24 * 1024 * 1024),
        name="ctx_kv",
    )(x, mod, g, w_in, wkt)


def _conv_kernel(main_ref, prev_ref, next_ref, w_ref, b_ref, lng_ref, lnb_ref, grp_ref, o_ref,
                 buf_ref, cv_ref):
    t = pl.program_id(1)
    nt = pl.num_programs(1)
    tc = main_ref.shape[1]
    buf_ref[0:CONV_HALO, :] = jnp.where(t > 0, prev_ref[0], 0.0)
    buf_ref[CONV_HALO:CONV_HALO + tc, :] = main_ref[0]
    buf_ref[CONV_HALO + tc:, :] = jnp.where(t < nt - 1, next_ref[0], 0.0)
    base = CONV_HALO - CONV_WIDTH // 2

    def conv_chunk(r0):
        for lc in range(CONV_CH // CONV_LC):
            lanes = slice(lc * CONV_LC, (lc + 1) * CONV_LC)
            acc = None
            for s in range(V7X_SUBLANES):
                part = None
                for a in range((base + CONV_WIDTH - 1) // V7X_SUBLANES + 1):
                    k = V7X_SUBLANES * a + s - base
                    if 0 <= k < CONV_WIDTH:
                        win = buf_ref[r0 + V7X_SUBLANES * a:r0 + V7X_SUBLANES * a + CONV_RC + V7X_SUBLANES, lanes]
                        term = w_ref[k:k + 1, lanes] * win
                        part = term if part is None else part + term
                shifted = part[s:s + CONV_RC, :]
                acc = shifted if acc is None else acc + shifted
            cv_ref[r0:r0 + CONV_RC, lanes] = acc + b_ref[:, lanes]

    grp = grp_ref[...]
    inv_n = 1.0 / (CONV_CH // CONV_GROUPS)

    def group_mean(z):
        hi = z.astype(_BF16)
        lo = (z - hi.astype(_F32)).astype(_BF16)
        return (_dot(hi, grp) + _dot(lo, grp)) * inv_n

    def group_norm(rows):
        v = cv_ref[rows, :]
        dlt = v - group_mean(v)
        var = _dot((dlt * dlt).astype(_BF16), grp) * inv_n
        vn = dlt * lax.rsqrt(var + EPS) * lng_ref[...] + lnb_ref[...]
        o_ref[0, rows, :] = _silu(vn).astype(_BF16)

    for blk in range(tc // CONV_NORM_ROWS):
        for r0 in range(blk * CONV_NORM_ROWS, (blk + 1) * CONV_NORM_ROWS, CONV_RC):
            conv_chunk(r0)
        group_norm(slice(blk * CONV_NORM_ROWS, (blk + 1) * CONV_NORM_ROWS))


def _conformer_conv(glu, dw_w, dw_b, ln_g, ln_b):
    nb, t, ch = glu.shape
    tc = CONV_TC
    hb = tc // CONV_HALO
    nh = t // CONV_HALO
    gid = np.arange(ch) // (ch // CONV_GROUPS)
    grp = jnp.asarray(gid[:, None] == gid[None, :], _BF16)
    row = lambda a: a.reshape(1, ch)
    rspec = pl.BlockSpec((1, ch), lambda b, i: (0, 0))
    return pl.pallas_call(
        _conv_kernel,
        grid=(nb, t // tc),
        in_specs=[
            pl.BlockSpec((1, tc, ch), lambda b, i: (b, i, 0)),
            pl.BlockSpec((1, CONV_HALO, ch), lambda b, i: (b, jnp.maximum(i * hb - 1, 0), 0)),
            pl.BlockSpec((1, CONV_HALO, ch), lambda b, i: (b, jnp.minimum((i + 1) * hb, nh - 1), 0)),
            pl.BlockSpec((CONV_WIDTH, ch), lambda b, i: (0, 0)),
            rspec, rspec, rspec,
            pl.BlockSpec((ch, ch), lambda b, i: (0, 0)),
        ],
        out_specs=pl.BlockSpec((1, tc, ch), lambda b, i: (b, i, 0)),
        out_shape=jax.ShapeDtypeStruct((nb, t, ch), _BF16),
        scratch_shapes=[pltpu.VMEM((tc + 2 * CONV_HALO, ch), _F32), pltpu.VMEM((tc, ch), _F32)],
        compiler_params=_params(2, 32 * 1024 * 1024),
        name="conformer_conv",
    )(glu, glu, glu, dw_w, row(dw_b), row(ln_g), row(ln_b), grp)


def _na_bias_table(rpb):
    rows = GRID_W
    pairs = rows // NA_QROWS
    reps = [0, 1, 2, pairs - 2, pairs - 1]
    heads, n_rr, n_cc = rpb.shape
    off = np.arange(GRID_W)[None, :] - np.arange(GRID_W)[:, None] + NA_WIN_C - 1
    onehot = (off[None] == np.arange(n_cc)[:, None, None]).reshape(n_cc, GRID_W * GRID_W)
    toe = jnp.einsum("hrk,kn->hrn", rpb.astype(_F32), jnp.asarray(onehot * LOG2E, _F32),
                     precision=lax.Precision.HIGHEST).reshape(heads, n_rr, GRID_W, GRID_W)
    toe2 = jnp.concatenate([toe, toe], axis=-1)

    def table_kernel(toe_ref, o_ref):
        t = pl.program_id(1)
        p = jnp.where(t < len(reps) // 2 + 1, t, t + pairs - len(reps))
        kb = jnp.clip(NA_QROWS * p - NA_WIN_R // 2, 0, rows - NA_KROWS)
        lane = lax.broadcasted_iota(jnp.int32, (GRID_W, 2 * GRID_W), 1)
        w = lax.broadcasted_iota(jnp.int32, (GRID_W, 2 * GRID_W), 0)
        c = jnp.where(lane < GRID_W, lane, lane - GRID_W)
        cs = jnp.clip(w - NA_WIN_C // 2, 0, GRID_W - NA_WIN_C)
        col_ok = jnp.logical_and(c >= cs, c < cs + NA_WIN_C)
        for qr in range(NA_QROWS):
            r = NA_QROWS * p + qr
            rs = jnp.clip(r - NA_WIN_R // 2, 0, rows - NA_WIN_R)
            for j in range(NA_KROWS // 2):
                halves = []
                for i in (2 * j, 2 * j + 1):
                    krow = kb + i
                    row_ok = jnp.logical_and(krow >= rs, krow < rs + NA_WIN_R)
                    rr = jnp.clip(krow - r + NA_WIN_R - 1, 0, n_rr - 1)
                    halves.append(jnp.where(jnp.logical_and(col_ok, row_ok), toe_ref[0, rr], NA_NEG))
                o_ref[0, 0, qr * GRID_W:(qr + 1) * GRID_W, 2 * j * GRID_W:2 * (j + 1) * GRID_W] = jnp.where(
                    lane < GRID_W, halves[0], halves[1])

    return pl.pallas_call(
        table_kernel,
        grid=(heads, len(reps)),
        in_specs=[pl.BlockSpec((1, n_rr, GRID_W, 2 * GRID_W), lambda h, t: (h, 0, 0, 0))],
        out_specs=pl.BlockSpec((1, 1, NA_TQ, NA_TK), lambda h, t: (t, h, 0, 0)),
        out_shape=jax.ShapeDtypeStruct((len(reps), heads, NA_TQ, NA_TK), _F32),
        compiler_params=_params(2, 16 * 1024 * 1024),
        name="na_bias_table",
    )(toe2)


def _na_kernel(q_ref, kt_ref, v_ref, kct_ref, vc_ref, *rest):
    bias_refs, o_ref = rest[:-1], rest[-1]
    step = pl.program_id(1)
    n_blocks = kt_ref.shape[1]
    lane = lax.broadcasted_iota(jnp.int32, (NA_TQ, 2 * NA_HEAD_DIM), 1)
    first_head = lane < NA_HEAD_DIM
    n_pairs = len(bias_refs)

    def scores(u):
        pp, h = divmod(u, NA_HEADS)
        jb = jnp.clip(step * n_pairs + pp - (NA_WIN_R // 2) // NA_QROWS, 0, n_blocks - NA_KROWS // NA_QROWS)
        ls = slice((h // 2) * 2 * NA_HEAD_DIM, (h // 2 + 1) * 2 * NA_HEAD_DIM)
        q2 = q_ref[0, pp * NA_TQ:(pp + 1) * NA_TQ, ls]
        qh = jnp.where(first_head if h % 2 == 0 else jnp.logical_not(first_head), q2, jnp.zeros_like(q2))
        kt_win = jnp.concatenate([kt_ref[0, jb + j, ls, :] for j in range(NA_KROWS // NA_QROWS)], axis=1)
        return _dot(qh, kt_win) + bias_refs[pp][0, h], _dot(qh, kct_ref[0, ls, :])

    pending = scores(0)
    tiles = []
    for u in range(n_pairs * NA_HEADS):
        pp, h = divmod(u, NA_HEADS)
        s_w, s_c = pending
        if u + 1 < n_pairs * NA_HEADS:
            pending = scores(u + 1)
        m = jnp.maximum(jnp.max(s_w, axis=-1, keepdims=True), jnp.max(s_c, axis=-1, keepdims=True))
        e_w = jnp.exp2(s_w - m).astype(_BF16)
        e_c = jnp.exp2(s_c - m).astype(_BF16)
        jb = jnp.clip(step * n_pairs + pp - (NA_WIN_R // 2) // NA_QROWS, 0, n_blocks - NA_KROWS // NA_QROWS)
        win_rows = pl.ds(pl.multiple_of(jb * NA_TQ, NA_TQ), NA_TK)
        hl = slice(h * NA_VTILE, (h + 1) * NA_VTILE)
        tiles.append(_dot(e_w, v_ref[0, win_rows, hl]) + _dot(e_c, vc_ref[0, :, hl]))
        if h % 2 == 1:
            num = jnp.where(first_head, tiles[u - 1], tiles[u])
            den = jnp.where(first_head, pltpu.roll(tiles[u - 1], NA_HEAD_DIM, 1),
                            pltpu.roll(tiles[u], NA_HEAD_DIM, 1))
            ls = slice((h // 2) * 2 * NA_HEAD_DIM, (h // 2 + 1) * 2 * NA_HEAD_DIM)
            o_ref[0, pp * NA_TQ:(pp + 1) * NA_TQ, ls] = (num / den).astype(_BF16)


def _neighbourhood_attention(q, kt, v, kct, vc, bias):
    nb, t, width = q.shape
    n_blocks = kt.shape[1]
    lc = vc.shape[1]
    pairs = t // NA_TQ
    n_types = bias.shape[0]

    def bias_spec(pp):
        def index(b, s):
            p = s * NA_STEP_PAIRS + pp
            edge = (n_types - 1) // 2
            kind = jnp.where(p < edge, p, jnp.where(p >= pairs - edge, p - (pairs - n_types), edge))
            return (kind, 0, 0, 0)
        return pl.BlockSpec((1,) + bias.shape[1:], index)

    tq = NA_STEP_PAIRS * NA_TQ
    vmem = (2 * (_nbytes((t, width), _BF16) + _nbytes(v.shape[1:], _BF16)
                 + NA_STEP_PAIRS * _nbytes(bias.shape[1:], _F32)) + 16 * 1024 * 1024)
    return pl.pallas_call(
        _na_kernel,
        grid=(nb, pairs // NA_STEP_PAIRS),
        in_specs=[
            pl.BlockSpec((1, tq, width), lambda b, s: (b, s, 0)),
            pl.BlockSpec((1, n_blocks, width, NA_TQ), lambda b, s: (b, 0, 0, 0)),
            pl.BlockSpec((1,) + v.shape[1:], lambda b, s: (b, 0, 0)),
            pl.BlockSpec((1, width, lc), lambda b, s: (b, 0, 0)),
            pl.BlockSpec((1,) + vc.shape[1:], lambda b, s: (b, 0, 0)),
        ] + [bias_spec(pp) for pp in range(NA_STEP_PAIRS)],
        out_specs=pl.BlockSpec((1, tq, width), lambda b, s: (b, s, 0)),
        out_shape=jax.ShapeDtypeStruct((nb, t, width), _BF16),
        compiler_params=_params(2, vmem),
        name="na_attention",
    )(q, kt, v, kct, vc, *([bias] * NA_STEP_PAIRS))


def _odd_kernel(x_ref, xp_ref, xn_ref, mod_ref, g_ref, win_ref, cw_ref, wo_ref, o_ref):
    t = pl.program_id(1)
    nt = pl.num_programs(1)
    x = x_ref[0]
    tm = x.shape[0]
    n = tm + 2 * ODD_HALO
    xa = jnp.concatenate([xp_ref[0], x, xn_ref[0]], axis=0)
    ha = _pre(xa, mod_ref, g_ref[0:1, :]).astype(_BF16)
    d = x.shape[1]
    cz = _dot(ha, win_ref[0, :, d:2 * d]) * _dot(ha, win_ref[0, :, 2 * d:])
    row = lax.broadcasted_iota(jnp.int32, (n, 1), 0)
    inside = jnp.logical_and(jnp.logical_or(row >= ODD_HALO, t > 0),
                             jnp.logical_or(row < tm + ODD_HALO, t < nt - 1))
    cz = jnp.where(inside, cz, 0.0)
    mid = slice(ODD_HALO, ODD_HALO + tm)
    y = (cw_ref[0:1, :] * pltpu.roll(cz, 1, 0)[mid]
         + cw_ref[1:2, :] * cz[mid]
         + cw_ref[2:3, :] * pltpu.roll(cz, n - 1, 0)[mid])
    z = (_dot(ha[mid], win_ref[0, :, :d]) * y).astype(_BF16)
    out = _dot(z, wo_ref[0])
    o_ref[0] = x + _post(out, mod_ref, g_ref[1:2, :])


def _odd_mixer(x, mod, g, w_in, cw, wo):
    nb, t, d = x.shape
    tm = PROJ_TM
    hb = tm // ODD_HALO
    nh = t // ODD_HALO
    wspec = lambda w: pl.BlockSpec(w.shape, lambda b, i: (0,) * w.ndim)
    vmem = (2 * (_nbytes(w_in.shape, _BF16) + _nbytes(wo.shape, _BF16))
            + 12 * _nbytes((tm + 2 * ODD_HALO, d), _F32) + 8 * 1024 * 1024)
    return pl.pallas_call(
        _odd_kernel,
        grid=(nb, t // tm),
        in_specs=[
            pl.BlockSpec((1, tm, d), lambda b, i: (b, i, 0)),
            pl.BlockSpec((1, ODD_HALO, d), lambda b, i: (b, jnp.maximum(i * hb - 1, 0), 0)),
            pl.BlockSpec((1, ODD_HALO, d), lambda b, i: (b, jnp.minimum((i + 1) * hb, nh - 1), 0)),
            pl.BlockSpec((1, 3, d), lambda b, i: (b, 0, 0)),
            pl.BlockSpec((2, d), lambda b, i: (0, 0)),
            wspec(w_in), wspec(cw), wspec(wo),
        ],
        out_specs=pl.BlockSpec((1, tm, d), lambda b, i: (b, i, 0)),
        out_shape=jax.ShapeDtypeStruct(x.shape, _F32),
        compiler_params=_params(2, vmem),
        name="odd_mixer",
    )(x, x, x, mod, g, w_in, cw, wo)


def kernel(x, c, ctx, c_ctx, w_mod, b_mod, norm_g, ff1_w_gu, ff1_w_down, ff2_w_gu, ff2_w_down,
           ev_w_in, ev_w_out, ev_dw_w, ev_dw_b, ev_ln_g, ev_ln_b, ev_rpb, od_w_in, od_conv_w, od_w_out):
    bsz, seq, d = x.shape
    lc = ctx.shape[1]
    ctx_row = bsz
    cond = jnp.concatenate([c, c_ctx[None, :], jnp.zeros((MOD_ROWS - bsz - 1, d), _F32)], axis=0)
    mods = _ada_mod(cond, w_mod, b_mod).reshape(w_mod.shape[0], MOD_ROWS, N_MOD, d)

    ff1_gu, ff1_dn = _to_bf16(ff1_w_gu), _to_bf16(ff1_w_down)
    ff2_gu, ff2_dn = _to_bf16(ff2_w_gu), _to_bf16(ff2_w_down)
    ev_in, ev_out = _to_bf16(ev_w_in), _to_bf16(ev_w_out)
    od_in, od_out = _to_bf16(od_w_in), _to_bf16(od_w_out)

    m, g = mods[0], norm_g[0]
    x = _ffn(x, m[:, 0:3], g[0:2], ff1_gu, ff1_dn, 0)
    x_ctx = _ffn(ctx.reshape(1, bsz * lc, d), m[:, 0:3], g[0:2], ff1_gu, ff1_dn, 0, shared_row=ctx_row)
    x_ctx = x_ctx.reshape(bsz, lc, d)

    wkt = ev_in[0, :, 3 * CONV_CH:4 * CONV_CH].T
    glu, q, kt, v = _even_in(x, m[:, 3:6], g[2:3], ev_in[:1], wkt)
    kct, vc = _ctx_kv(x_ctx, m[:, 3:6], g[2:3], ev_in[:1], wkt, shared_row=ctx_row)
    ya = _conformer_conv(glu, ev_dw_w[0], ev_dw_b[0], ev_ln_g[0], ev_ln_b[0])
    yb = _neighbourhood_attention(q, kt, v, kct, vc, _na_bias_table(ev_rpb[0]))
    x = _ffn(x, m[:, 6:9], g[4:6], ff2_gu, ff2_dn, 0, mixer=(ya, yb, m[:, 3:6], g[3:4], ev_out[:1]))

    m, g = mods[1], norm_g[1]
    x = _ffn(x, m[:, 0:3], g[0:2], ff1_gu, ff1_dn, 1)
    x = _odd_mixer(x, m[:, 3:6], g[2:4], od_in[:1], od_conv_w[0], od_out[:1])
    x = _ffn(x, m[:, 6:9], g[4:6], ff2_gu, ff2_dn, 1)
    return x
```

```python
import functools
import math

import jax
import jax.numpy as jnp
import numpy as np
from jax import lax
from jax.experimental import pallas as pl
from jax.experimental.pallas import tpu as pltpu

D_MODEL = 1024
D_FF = 2816
N_MOD = 9
EPS = 1e-6
FFN_RES = 0.5
GRID_W = 64
CONV_CH = 512
CONV_GROUPS = 8
CONV_WIDTH = 31
NA_HEADS = 8
NA_HEAD_DIM = 64
NA_WIDTH = NA_HEADS * NA_HEAD_DIM
NA_WIN_R = 8
NA_WIN_C = 16
SC_CONV = 3

V7X_LANES = 128
V7X_SUBLANES = 8
V7X_VMEM_BYTES = 64 * 1024 * 1024
BF16_SUBLANES = 16

MOD_ROWS = 16
MOD_TN = 1536
CAST_BLOCK_BYTES = 6 * 1024 * 1024
FFN_TM = 1024
FFN_SUB = 512
FFN_PIECES = 8
FFN_FC = 256
FFN_NCH = D_FF // FFN_FC
PROJ_TM = 512
EVEN_TM = 1024
CONV_TC = 1024
CONV_HALO = 16
CONV_RC = 64
CONV_LC = 256
CONV_NORM_ROWS = 128
NA_QROWS = 2
NA_TQ = NA_QROWS * GRID_W
NA_KROWS = 10
NA_TK = NA_KROWS * GRID_W
NA_NEG = -1e30
LOG2E = math.log2(math.e)
NA_VTILE = 2 * NA_HEAD_DIM
NA_STEP_PAIRS = 4
ODD_HALO = 8

_BF16 = jnp.bfloat16
_F32 = jnp.float32


def _dot(a, b):
    return jnp.dot(a, b, preferred_element_type=_F32)


def _dot_nt(a, b):
    return lax.dot_general(a, b, (((1,), (1,)), ((), ())), preferred_element_type=_F32)


def _silu(x):
    return x * jax.nn.sigmoid(x)


def _rms(x, row):
    return x * lax.rsqrt(jnp.mean(x * x, axis=-1, keepdims=True) + EPS) * row


def _pre(x, mod_ref, g):
    return _rms(x, g * (1.0 + mod_ref[0, 1:2, :])) + mod_ref[0, 0:1, :]


def _post(y, mod_ref, g, weight=1.0):
    return _rms(y, (weight * mod_ref[0, 2:3, :]) * g)


def _params(n_axes, vmem_bytes):
    limit = min(int(vmem_bytes), V7X_VMEM_BYTES - 4 * 1024 * 1024)
    return pltpu.CompilerParams(dimension_semantics=("arbitrary",) * n_axes, vmem_limit_bytes=limit)


def _nbytes(shape, dtype):
    return int(np.prod(shape)) * jnp.dtype(dtype).itemsize


def _mod_kernel(cond_ref, w_ref, b_ref, o_ref):
    a = _silu(cond_ref[...]).astype(_BF16)
    o_ref[0] = _dot(a, w_ref[0].astype(_BF16)) + b_ref[0]


def _ada_mod(cond, w_mod, b_mod):
    depth, d, n = w_mod.shape
    vmem = 2 * _nbytes((d, MOD_TN), _F32) + _nbytes((d, MOD_TN), _BF16) + 8 * 1024 * 1024
    return pl.pallas_call(
        _mod_kernel,
        grid=(depth, n // MOD_TN),
        in_specs=[
            pl.BlockSpec((MOD_ROWS, d), lambda i, j: (0, 0)),
            pl.BlockSpec((1, d, MOD_TN), lambda i, j: (i, 0, j)),
            pl.BlockSpec((1, 1, MOD_TN), lambda i, j: (i, 0, j)),
        ],
        out_specs=pl.BlockSpec((1, MOD_ROWS, MOD_TN), lambda i, j: (i, 0, j)),
        out_shape=jax.ShapeDtypeStruct((depth, MOD_ROWS, n), _F32),
        compiler_params=_params(2, vmem),
        name="ada_mod",
    )(cond, w_mod, b_mod.reshape(depth, 1, n))


def _ffn_kernel(*refs, mixer_out):
    if mixer_out:
        (x_ref, mod_ref, g_ref, wgu_ref, wd_ref, ya_ref, yb_ref, mmod_ref, mg_ref, wo_ref,
         o_ref, h_ref, xin_ref) = refs
    else:
        x_ref, mod_ref, g_ref, wgu_ref, wd_ref, o_ref, h_ref = refs
        xin_ref = x_ref.at[0]
    acc_ref = o_ref.at[0]
    tm = x_ref.shape[1]
    sizes = (FFN_SUB,) * (tm // FFN_SUB) if tm % FFN_SUB == 0 else (tm,)
    starts = [sum(sizes[:s]) for s in range(len(sizes))]
    subs = [slice(a, a + n) for a, n in zip(starts, sizes)]
    if mixer_out:
        for rows in subs:
            y = (_dot(ya_ref[0, rows, :], wo_ref[0, :CONV_CH, :])
                 + _dot(yb_ref[0, rows, :], wo_ref[0, CONV_CH:, :]))
            xin_ref[rows, :] = x_ref[0, rows, :] + _post(y, mmod_ref, mg_ref[...])
    always = pl.program_id(0) >= 0
    pieces = min(FFN_PIECES, FFN_NCH)

    def paced(value, anchor):
        return value if anchor is None else jnp.where(always, value, anchor[0:1, 0:1])

    def pre(rows, anchor=None):
        h_ref[rows, :] = _pre(paced(xin_ref[rows, :], anchor), mod_ref, g_ref[0:1, :]).astype(_BF16)

    def post(rows, anchor=None):
        y = _post(paced(acc_ref[rows, :], anchor), mod_ref, g_ref[1:2, :], FFN_RES)
        o_ref[0, rows, :] = xin_ref[rows, :] + y

    def piece_rows(s, k):
        piece = sizes[s] // pieces
        return slice(starts[s] + k * piece, starts[s] + (k + 1) * piece)

    pre(subs[0])
    for s, rows in enumerate(subs):
        for c in range(FFN_NCH):
            h = h_ref[rows, :]
            gate = _dot(h, wgu_ref[0, :, c * FFN_FC:(c + 1) * FFN_FC])
            up = _dot(h, wgu_ref[0, :, D_FF + c * FFN_FC:D_FF + (c + 1) * FFN_FC])
            act = (_silu(gate) * up).astype(_BF16)
            part = _dot(act, wd_ref[0, c * FFN_FC:(c + 1) * FFN_FC, :])
            if c == 0:
                acc_ref[rows, :] = part
            else:
                acc_ref[rows, :] += part
            if c < pieces:
                if s + 1 < len(subs):
                    pre(piece_rows(s + 1, c), anchor=part)
                if s > 0:
                    post(piece_rows(s - 1, c), anchor=part)
    post(subs[-1])


def _ffn(x, mod, g, wgu, wd, layer, shared_row=None, mixer=None):
    nb, t, d = x.shape
    tm = min(FFN_TM, t)
    mod_map = (lambda b, i: (b, 0, 0)) if shared_row is None else (lambda b, i: (shared_row, 0, 0))
    resident = pl.Buffered(1)
    wspec = lambda w: pl.BlockSpec((1,) + w.shape[1:], lambda b, i: (layer, 0, 0), pipeline_mode=resident)
    tok = lambda width: pl.BlockSpec((1, tm, width), lambda b, i: (b, i, 0))
    in_specs = [tok(d), pl.BlockSpec((1, 3, d), mod_map), pl.BlockSpec((2, d), lambda b, i: (0, 0)),
                wspec(wgu), wspec(wd)]
    args = [x, mod, g, wgu, wd]
    scratch = [pltpu.VMEM((tm, d), _BF16)]
    vmem = (_nbytes(wgu.shape[1:], _BF16) + _nbytes(wd.shape[1:], _BF16) + 4 * _nbytes((tm, d), _F32)
            + _nbytes((tm, d), _BF16) + 8 * 1024 * 1024)
    if mixer is not None:
        ya, yb, mmod, mg, wo = mixer
        in_specs += [tok(ya.shape[-1]), tok(yb.shape[-1]), pl.BlockSpec((1, 3, d), lambda b, i: (b, 0, 0)),
                     pl.BlockSpec((1, d), lambda b, i: (0, 0)),
                     pl.BlockSpec((1,) + wo.shape[1:], lambda b, i: (0, 0, 0), pipeline_mode=resident)]
        args += [ya, yb, mmod, mg, wo]
        scratch.append(pltpu.VMEM((tm, d), _F32))
        vmem += (_nbytes(wo.shape[1:], _BF16) + _nbytes((tm, d), _F32)
                 + 2 * _nbytes((tm, ya.shape[-1] + yb.shape[-1]), _BF16))
    return pl.pallas_call(
        functools.partial(_ffn_kernel, mixer_out=mixer is not None),
        grid=(nb, t // tm),
        in_specs=in_specs,
        out_specs=tok(d),
        out_shape=jax.ShapeDtypeStruct(x.shape, _F32),
        scratch_shapes=scratch,
        compiler_params=_params(2, vmem),
        name="ffn",
    )(*args)


def _cast_kernel(w_ref, o_ref):
    o_ref[...] = w_ref[...].astype(_BF16)


def _cast_rows(r, c):
    limit = CAST_BLOCK_BYTES // (c * jnp.dtype(_F32).itemsize)
    return max(tr for tr in range(BF16_SUBLANES, r + 1, BF16_SUBLANES) if r % tr == 0 and tr <= limit)


def _to_bf16(w):
    layers, r, c = w.shape
    tr = _cast_rows(r, c)
    spec = pl.BlockSpec((1, tr, c), lambda l, i: (l, i, 0))
    return pl.pallas_call(
        _cast_kernel,
        grid=(layers, r // tr),
        in_specs=[spec],
        out_specs=spec,
        out_shape=pltpu.HBM(w.shape, _BF16),
        compiler_params=_params(2, 4 * CAST_BLOCK_BYTES + 8 * 1024 * 1024),
        name="to_bf16",
    )(w)


def _store_values_with_ones(v, v_ref):
    lane = lax.broadcasted_iota(jnp.int32, (v.shape[0], NA_VTILE), 1)
    first_head = lane < NA_HEAD_DIM
    for j in range(NA_HEADS // 2):
        pair = v[:, j * NA_VTILE:(j + 1) * NA_VTILE]
        v_ref[0, :, 2 * j * NA_VTILE:(2 * j + 1) * NA_VTILE] = jnp.where(first_head, pair, 1.0).astype(_BF16)
        v_ref[0, :, (2 * j + 1) * NA_VTILE:(2 * j + 2) * NA_VTILE] = jnp.where(first_head, 1.0, pair).astype(_BF16)


def _even_in_kernel(x_ref, mod_ref, g_ref, win_ref, wkt_ref, glu_ref, q_ref, kt_ref, v_ref):
    col = lambda j: slice(j * CONV_CH, (j + 1) * CONV_CH)
    h = _pre(x_ref[0], mod_ref, g_ref[...]).astype(_BF16)
    glu_ref[0] = _dot(h, win_ref[0, :, col(0)]) * jax.nn.sigmoid(_dot(h, win_ref[0, :, col(1)]))
    q_ref[0] = (_dot(h, win_ref[0, :, col(2)]) * (NA_HEAD_DIM ** -0.5 * LOG2E)).astype(_BF16)
    _store_values_with_ones(_dot(h, win_ref[0, :, col(4)]), v_ref)
    kt = _dot_nt(wkt_ref[...], h)
    for j in range(kt.shape[1] // NA_TQ):
        kt_ref[0, j] = kt[:, j * NA_TQ:(j + 1) * NA_TQ].astype(_BF16)


def _even_in(x, mod, g, w_in, wkt):
    nb, t, d = x.shape
    tm = EVEN_TM
    vmem = 2 * _nbytes(w_in.shape, _BF16) + 8 * _nbytes((tm, d), _F32) + 16 * 1024 * 1024
    return pl.pallas_call(
        _even_in_kernel,
        grid=(nb, t // tm),
        in_specs=[
            pl.BlockSpec((1, tm, d), lambda b, i: (b, i, 0)),
            pl.BlockSpec((1, 3, d), lambda b, i: (b, 0, 0)),
            pl.BlockSpec((1, d), lambda b, i: (0, 0)),
            pl.BlockSpec(w_in.shape, lambda b, i: (0, 0, 0)),
            pl.BlockSpec(wkt.shape, lambda b, i: (0, 0)),
        ],
        out_specs=[
            pl.BlockSpec((1, tm, CONV_CH), lambda b, i: (b, i, 0)),
            pl.BlockSpec((1, tm, NA_WIDTH), lambda b, i: (b, i, 0)),
            pl.BlockSpec((1, tm // NA_TQ, NA_WIDTH, NA_TQ), lambda b, i: (b, i, 0, 0)),
            pl.BlockSpec((1, tm, NA_HEADS * NA_VTILE), lambda b, i: (b, i, 0)),
        ],
        out_shape=[
            jax.ShapeDtypeStruct((nb, t, CONV_CH), _F32),
            jax.ShapeDtypeStruct((nb, t, NA_WIDTH), _BF16),
            jax.ShapeDtypeStruct((nb, t // NA_TQ, NA_WIDTH, NA_TQ), _BF16),
            jax.ShapeDtypeStruct((nb, t, NA_HEADS * NA_VTILE), _BF16),
        ],
        compiler_params=_params(2, vmem),
        name="even_in",
    )(x, mod, g, w_in, wkt)


def _ctx_kv_kernel(x_ref, mod_ref, g_ref, win_ref, wkt_ref, kt_ref, v_ref):
    h = _pre(x_ref[0], mod_ref, g_ref[...]).astype(_BF16)
    kt_ref[0] = _dot_nt(wkt_ref[...], h).astype(_BF16)
    _store_values_with_ones(_dot(h, win_ref[0, :, 4 * CONV_CH:]), v_ref)


def _ctx_kv(x, mod, g, w_in, wkt, shared_row):
    nb, t, d = x.shape
    return pl.pallas_call(
        _ctx_kv_kernel,
        grid=(nb,),
        in_specs=[
            pl.BlockSpec((1, t, d), lambda b: (b, 0, 0)),
            pl.BlockSpec((1, 3, d), lambda b: (shared_row, 0, 0)),
            pl.BlockSpec((1, d), lambda b: (0, 0)),
            pl.BlockSpec(w_in.shape, lambda b: (0, 0, 0)),
            pl.BlockSpec(wkt.shape, lambda b: (0, 0)),
        ],
        out_specs=[
            pl.BlockSpec((1, NA_WIDTH, t), lambda b: (b, 0, 0)),
            pl.BlockSpec((1, t, NA_HEADS * NA_VTILE), lambda b: (b, 0, 0)),
        ],
        out_shape=[
            pltpu.HBM((nb, NA_WIDTH, t), _BF16),
            pltpu.HBM((nb, t, NA_HEADS * NA_VTILE), _BF16),
        ],
        compiler_params=_params(1, 24 * 1024 * 1024),
        name="ctx_kv",
    )(x, mod, g, w_in, wkt)


def _conv_kernel(main_ref, prev_ref, next_ref, w_ref, b_ref, lng_ref, lnb_ref, grp_ref, o_ref,
                 buf_ref, cv_ref):
    t = pl.program_id(1)
    nt = pl.num_programs(1)
    tc = main_ref.shape[1]
    buf_ref[0:CONV_HALO, :] = jnp.where(t > 0, prev_ref[0], 0.0)
    buf_ref[CONV_HALO:CONV_HALO + tc, :] = main_ref[0]
    buf_ref[CONV_HALO + tc:, :] = jnp.where(t < nt - 1, next_ref[0], 0.0)
    base = CONV_HALO - CONV_WIDTH // 2

    def conv_chunk(r0):
        for lc in range(CONV_CH // CONV_LC):
            lanes = slice(lc * CONV_LC, (lc + 1) * CONV_LC)
            acc = None
            for s in range(V7X_SUBLANES):
                part = None
                for a in range((base + CONV_WIDTH - 1) // V7X_SUBLANES + 1):
                    k = V7X_SUBLANES * a + s - base
                    if 0 <= k < CONV_WIDTH:
                        win = buf_ref[r0 + V7X_SUBLANES * a:r0 + V7X_SUBLANES * a + CONV_RC + V7X_SUBLANES, lanes]
                        term = w_ref[k:k + 1, lanes] * win
                        part = term if part is None else part + term
                shifted = part[s:s + CONV_RC, :]
                acc = shifted if acc is None else acc + shifted
            cv_ref[r0:r0 + CONV_RC, lanes] = acc + b_ref[:, lanes]

    grp = grp_ref[...]
    inv_n = 1.0 / (CONV_CH // CONV_GROUPS)

    def group_mean(z):
        hi = z.astype(_BF16)
        lo = (z - hi.astype(_F32)).astype(_BF16)
        return (_dot(hi, grp) + _dot(lo, grp)) * inv_n

    def group_norm(rows):
        v = cv_ref[rows, :]
        dlt = v - group_mean(v)
        var = _dot((dlt * dlt).astype(_BF16), grp) * inv_n
        vn = dlt * lax.rsqrt(var + EPS) * lng_ref[...] + lnb_ref[...]
        o_ref[0, rows, :] = _silu(vn).astype(_BF16)

    for blk in range(tc // CONV_NORM_ROWS):
        for r0 in range(blk * CONV_NORM_ROWS, (blk + 1) * CONV_NORM_ROWS, CONV_RC):
            conv_chunk(r0)
        group_norm(slice(blk * CONV_NORM_ROWS, (blk + 1) * CONV_NORM_ROWS))


def _conformer_conv(glu, dw_w, dw_b, ln_g, ln_b):
    nb, t, ch = glu.shape
    tc = CONV_TC
    hb = tc // CONV_HALO
    nh = t // CONV_HALO
    gid = np.arange(ch) // (ch // CONV_GROUPS)
    grp = jnp.asarray(gid[:, None] == gid[None, :], _BF16)
    row = lambda a: a.reshape(1, ch)
    rspec = pl.BlockSpec((1, ch), lambda b, i: (0, 0))
    return pl.pallas_call(
        _conv_kernel,
        grid=(nb, t // tc),
        in_specs=[
            pl.BlockSpec((1, tc, ch), lambda b, i: (b, i, 0)),
            pl.BlockSpec((1, CONV_HALO, ch), lambda b, i: (b, jnp.maximum(i * hb - 1, 0), 0)),
            pl.BlockSpec((1, CONV_HALO, ch), lambda b, i: (b, jnp.minimum((i + 1) * hb, nh - 1), 0)),
            pl.BlockSpec((CONV_WIDTH, ch), lambda b, i: (0, 0)),
            rspec, rspec, rspec,
            pl.BlockSpec((ch, ch), lambda b, i: (0, 0)),
        ],
        out_specs=pl.BlockSpec((1, tc, ch), lambda b, i: (b, i, 0)),
        out_shape=jax.ShapeDtypeStruct((nb, t, ch), _BF16),
        scratch_shapes=[pltpu.VMEM((tc + 2 * CONV_HALO, ch), _F32), pltpu.VMEM((tc, ch), _F32)],
        compiler_params=_params(2, 32 * 1024 * 1024),
        name="conformer_conv",
    )(glu, glu, glu, dw_w, row(dw_b), row(ln_g), row(ln_b), grp)


def _na_bias_table(rpb):
    rows = GRID_W
    pairs = rows // NA_QROWS
    reps = [0, 1, 2, pairs - 2, pairs - 1]
    heads, n_rr, n_cc = rpb.shape
    off = np.arange(GRID_W)[None, :] - np.arange(GRID_W)[:, None] + NA_WIN_C - 1
    onehot = (off[None] == np.arange(n_cc)[:, None, None]).reshape(n_cc, GRID_W * GRID_W)
    toe = jnp.einsum("hrk,kn->hrn", rpb.astype(_F32), jnp.asarray(onehot * LOG2E, _F32),
                     precision=lax.Precision.HIGHEST).reshape(heads, n_rr, GRID_W, GRID_W)
    toe2 = jnp.concatenate([toe, toe], axis=-1)

    def table_kernel(toe_ref, o_ref):
        t = pl.program_id(1)
        p = jnp.where(t < len(reps) // 2 + 1, t, t + pairs - len(reps))
        kb = jnp.clip(NA_QROWS * p - NA_WIN_R // 2, 0, rows - NA_KROWS)
        lane = lax.broadcasted_iota(jnp.int32, (GRID_W, 2 * GRID_W), 1)
        w = lax.broadcasted_iota(jnp.int32, (GRID_W, 2 * GRID_W), 0)
        c = jnp.where(lane < GRID_W, lane, lane - GRID_W)
        cs = jnp.clip(w - NA_WIN_C // 2, 0, GRID_W - NA_WIN_C)
        col_ok = jnp.logical_and(c >= cs, c < cs + NA_WIN_C)
        for qr in range(NA_QROWS):
            r = NA_QROWS * p + qr
            rs = jnp.clip(r - NA_WIN_R // 2, 0, rows - NA_WIN_R)
            for j in range(NA_KROWS // 2):
                halves = []
                for i in (2 * j, 2 * j + 1):
                    krow = kb + i
                    row_ok = jnp.logical_and(krow >= rs, krow < rs + NA_WIN_R)
                    rr = jnp.clip(krow - r + NA_WIN_R - 1, 0, n_rr - 1)
                    halves.append(jnp.where(jnp.logical_and(col_ok, row_ok), toe_ref[0, rr], NA_NEG))
                o_ref[0, 0, qr * GRID_W:(qr + 1) * GRID_W, 2 * j * GRID_W:2 * (j + 1) * GRID_W] = jnp.where(
                    lane < GRID_W, halves[0], halves[1])

    return pl.pallas_call(
        table_kernel,
        grid=(heads, len(reps)),
        in_specs=[pl.BlockSpec((1, n_rr, GRID_W, 2 * GRID_W), lambda h, t: (h, 0, 0, 0))],
        out_specs=pl.BlockSpec((1, 1, NA_TQ, NA_TK), lambda h, t: (t, h, 0, 0)),
        out_shape=pltpu.HBM((len(reps), heads, NA_TQ, NA_TK), _F32),
        compiler_params=_params(2, 16 * 1024 * 1024),
        name="na_bias_table",
    )(toe2)


def _na_kernel(q_ref, kt_ref, v_ref, kct_ref, vc_ref, *rest):
    bias_refs, o_ref = rest[:-1], rest[-1]
    step = pl.program_id(1)
    n_blocks = kt_ref.shape[1]
    lane = lax.broadcasted_iota(jnp.int32, (NA_TQ, 2 * NA_HEAD_DIM), 1)
    first_head = lane < NA_HEAD_DIM
    n_pairs = len(bias_refs)

    def scores(u):
        pp, h = divmod(u, NA_HEADS)
        jb = jnp.clip(step * n_pairs + pp - (NA_WIN_R // 2) // NA_QROWS, 0, n_blocks - NA_KROWS // NA_QROWS)
        ls = slice((h // 2) * 2 * NA_HEAD_DIM, (h // 2 + 1) * 2 * NA_HEAD_DIM)
        q2 = q_ref[0, pp * NA_TQ:(pp + 1) * NA_TQ, ls]
        qh = jnp.where(first_head if h % 2 == 0 else jnp.logical_not(first_head), q2, jnp.zeros_like(q2))
        kt_win = jnp.concatenate([kt_ref[0, jb + j, ls, :] for j in range(NA_KROWS // NA_QROWS)], axis=1)
        return _dot(qh, kt_win) + bias_refs[pp][0, h], _dot(qh, kct_ref[0, ls, :])

    pending = scores(0)
    tiles = []
    for u in range(n_pairs * NA_HEADS):
        pp, h = divmod(u, NA_HEADS)
        s_w, s_c = pending
        if u + 1 < n_pairs * NA_HEADS:
            pending = scores(u + 1)
        m = jnp.maximum(jnp.max(s_w, axis=-1, keepdims=True), jnp.max(s_c, axis=-1, keepdims=True))
        e_w = jnp.exp2(s_w - m).astype(_BF16)
        e_c = jnp.exp2(s_c - m).astype(_BF16)
        jb = jnp.clip(step * n_pairs + pp - (NA_WIN_R // 2) // NA_QROWS, 0, n_blocks - NA_KROWS // NA_QROWS)
        win_rows = pl.ds(pl.multiple_of(jb * NA_TQ, NA_TQ), NA_TK)
        hl = slice(h * NA_VTILE, (h + 1) * NA_VTILE)
        tiles.append(_dot(e_w, v_ref[0, win_rows, hl]) + _dot(e_c, vc_ref[0, :, hl]))
        if h % 2 == 1:
            num = jnp.where(first_head, tiles[u - 1], tiles[u])
            den = jnp.where(first_head, pltpu.roll(tiles[u - 1], NA_HEAD_DIM, 1),
                            pltpu.roll(tiles[u], NA_HEAD_DIM, 1))
            ls = slice((h // 2) * 2 * NA_HEAD_DIM, (h // 2 + 1) * 2 * NA_HEAD_DIM)
            o_ref[0, pp * NA_TQ:(pp + 1) * NA_TQ, ls] = (num / den).astype(_BF16)


def _neighbourhood_attention(q, kt, v, kct, vc, bias):
    nb, t, width = q.shape
    n_blocks = kt.shape[1]
    lc = vc.shape[1]
    pairs = t // NA_TQ
    n_types = bias.shape[0]

    def bias_spec(pp):
        def index(b, s):
            p = s * NA_STEP_PAIRS + pp
            edge = (n_types - 1) // 2
            kind = jnp.where(p < edge, p, jnp.where(p >= pairs - edge, p - (pairs - n_types), edge))
            return (kind, 0, 0, 0)
        return pl.BlockSpec((1,) + bias.shape[1:], index)

    tq = NA_STEP_PAIRS * NA_TQ
    vmem = (2 * (_nbytes((t, width), _BF16) + _nbytes(v.shape[1:], _BF16)
                 + NA_STEP_PAIRS * _nbytes(bias.shape[1:], _F32)) + 16 * 1024 * 1024)
    return pl.pallas_call(
        _na_kernel,
        grid=(nb, pairs // NA_STEP_PAIRS),
        in_specs=[
            pl.BlockSpec((1, tq, width), lambda b, s: (b, s, 0)),
            pl.BlockSpec((1, n_blocks, width, NA_TQ), lambda b, s: (b, 0, 0, 0)),
            pl.BlockSpec((1,) + v.shape[1:], lambda b, s: (b, 0, 0)),
            pl.BlockSpec((1, width, lc), lambda b, s: (b, 0, 0)),
            pl.BlockSpec((1,) + vc.shape[1:], lambda b, s: (b, 0, 0)),
        ] + [bias_spec(pp) for pp in range(NA_STEP_PAIRS)],
        out_specs=pl.BlockSpec((1, tq, width), lambda b, s: (b, s, 0)),
        out_shape=jax.ShapeDtypeStruct((nb, t, width), _BF16),
        compiler_params=_params(2, vmem),
        name="na_attention",
    )(q, kt, v, kct, vc, *([bias] * NA_STEP_PAIRS))


def _odd_kernel(x_ref, xp_ref, xn_ref, mod_ref, g_ref, win_ref, cw_ref, wo_ref, o_ref):
    t = pl.program_id(1)
    nt = pl.num_programs(1)
    x = x_ref[0]
    tm = x.shape[0]
    n = tm + 2 * ODD_HALO
    xa = jnp.concatenate([xp_ref[0], x, xn_ref[0]], axis=0)
    ha = _pre(xa, mod_ref, g_ref[0:1, :]).astype(_BF16)
    d = x.shape[1]
    cz = _dot(ha, win_ref[0, :, d:2 * d]) * _dot(ha, win_ref[0, :, 2 * d:])
    row = lax.broadcasted_iota(jnp.int32, (n, 1), 0)
    inside = jnp.logical_and(jnp.logical_or(row >= ODD_HALO, t > 0),
                             jnp.logical_or(row < tm + ODD_HALO, t < nt - 1))
    cz = jnp.where(inside, cz, 0.0)
    mid = slice(ODD_HALO, ODD_HALO + tm)
    y = (cw_ref[0:1, :] * pltpu.roll(cz, 1, 0)[mid]
         + cw_ref[1:2, :] * cz[mid]
         + cw_ref[2:3, :] * pltpu.roll(cz, n - 1, 0)[mid])
    z = (_dot(ha[mid], win_ref[0, :, :d]) * y).astype(_BF16)
    out = _dot(z, wo_ref[0])
    o_ref[0] = x + _post(out, mod_ref, g_ref[1:2, :])


def _odd_mixer(x, mod, g, w_in, cw, wo):
    nb, t, d = x.shape
    tm = PROJ_TM
    hb = tm // ODD_HALO
    nh = t // ODD_HALO
    wspec = lambda w: pl.BlockSpec(w.shape, lambda b, i: (0,) * w.ndim)
    vmem = (2 * (_nbytes(w_in.shape, _BF16) + _nbytes(wo.shape, _BF16))
            + 12 * _nbytes((tm + 2 * ODD_HALO, d), _F32) + 8 * 1024 * 1024)
    return pl.pallas_call(
        _odd_kernel,
        grid=(nb, t // tm),
        in_specs=[
            pl.BlockSpec((1, tm, d), lambda b, i: (b, i, 0)),
            pl.BlockSpec((1, ODD_HALO, d), lambda b, i: (b, jnp.maximum(i * hb - 1, 0), 0)),
            pl.BlockSpec((1, ODD_HALO, d), lambda b, i: (b, jnp.minimum((i + 1) * hb, nh - 1), 0)),
            pl.BlockSpec((1, 3, d), lambda b, i: (b, 0, 0)),
            pl.BlockSpec((2, d), lambda b, i: (0, 0)),
            wspec(w_in), wspec(cw), wspec(wo),
        ],
        out_specs=pl.BlockSpec((1, tm, d), lambda b, i: (b, i, 0)),
        out_shape=jax.ShapeDtypeStruct(x.shape, _F32),
        compiler_params=_params(2, vmem),
        name="odd_mixer",
    )(x, x, x, mod, g, w_in, cw, wo)


def kernel(x, c, ctx, c_ctx, w_mod, b_mod, norm_g, ff1_w_gu, ff1_w_down, ff2_w_gu, ff2_w_down,
           ev_w_in, ev_w_out, ev_dw_w, ev_dw_b, ev_ln_g, ev_ln_b, ev_rpb, od_w_in, od_conv_w, od_w_out):
    bsz, seq, d = x.shape
    lc = ctx.shape[1]
    ctx_row = bsz
    cond = jnp.concatenate([c, c_ctx[None, :], jnp.zeros((MOD_ROWS - bsz - 1, d), _F32)], axis=0)
    mods = _ada_mod(cond, w_mod, b_mod).reshape(w_mod.shape[0], MOD_ROWS, N_MOD, d)

    ff1_gu, ff1_dn = _to_bf16(ff1_w_gu), _to_bf16(ff1_w_down)
    ff2_gu, ff2_dn = _to_bf16(ff2_w_gu), _to_bf16(ff2_w_down)
    ev_in, ev_out = _to_bf16(ev_w_in), _to_bf16(ev_w_out)
    od_in, od_out = _to_bf16(od_w_in), _to_bf16(od_w_out)

    m, g = mods[0], norm_g[0]
    x = _ffn(x, m[:, 0:3], g[0:2], ff1_gu, ff1_dn, 0)
    x_ctx = _ffn(ctx.reshape(1, bsz * lc, d), m[:, 0:3], g[0:2], ff1_gu, ff1_dn, 0, shared_row=ctx_row)
    x_ctx = x_ctx.reshape(bsz, lc, d)

    wkt = ev_in[0, :, 3 * CONV_CH:4 * CONV_CH].T
    glu, q, kt, v = _even_in(x, m[:, 3:6], g[2:3], ev_in[:1], wkt)
    kct, vc = _ctx_kv(x_ctx, m[:, 3:6], g[2:3], ev_in[:1], wkt, shared_row=ctx_row)
    ya = _conformer_conv(glu, ev_dw_w[0], ev_dw_b[0], ev_ln_g[0], ev_ln_b[0])
    yb = _neighbourhood_attention(q, kt, v, kct, vc, _na_bias_table(ev_rpb[0]))
    x = _ffn(x, m[:, 6:9], g[4:6], ff2_gu, ff2_dn, 0, mixer=(ya, yb, m[:, 3:6], g[3:4], ev_out[:1]))

    m, g = mods[1], norm_g[1]
    x = _ffn(x, m[:, 0:3], g[0:2], ff1_gu, ff1_dn, 1)
    x = _odd_mixer(x, m[:, 3:6], g[2:4], od_in[:1], od_conv_w[0], od_out[:1])
    x = _ffn(x, m[:, 6:9], g[4:6], ff2_gu, ff2_dn, 1)
    return x
```
